```python
import math
import jax, jax.numpy as jnp
from jax import lax
import numpy as np

D_MODEL = 2048
BATCH = 2
SEQ = 8192
DEPTH = 2

POOL_WINDOWS = (2, 4, 8, 16)
POOL_WIDTH = D_MODEL // 2
POOL_GROUP = POOL_WIDTH // len(POOL_WINDOWS)
ATTN_HEADS = 8
ATTN_HEAD_DIM = (D_MODEL // 2) // ATTN_HEADS
ATTN_WIDTH = ATTN_HEADS * ATTN_HEAD_DIM
IDX_HEADS = 16
IDX_HEAD_DIM = 64
TOPK_MAX = 256
Q_BLOCK = 128
AB_SIZES = (POOL_WIDTH, ATTN_WIDTH, ATTN_WIDTH, ATTN_WIDTH,
            IDX_HEADS * IDX_HEAD_DIM, IDX_HEAD_DIM, IDX_HEADS)
AB_IN = sum(AB_SIZES)
CONV_WIDTH = 3
N_GROUPS = 4
EXPERTS_PER_GROUP = 8
N_EXPERTS = N_GROUPS * EXPERTS_PER_GROUP
EXPERT_TOP_K = 2
D_FF_EXPERT = 512
N_EVEN = (DEPTH + 1) // 2
N_ODD = DEPTH // 2
ALPHA = (2 * DEPTH) ** 0.25
BETA = (8 * DEPTH) ** -0.25
LN_EPS = 1e-5

kernel_name = "hybrid_pool_dsa_shortconv_hmoe_deepnorm"


def layer_norm(x, g, b):
    xf = x.astype(jnp.float32)
    mu = jnp.mean(xf, axis=-1, keepdims=True)
    var = jnp.mean(jnp.square(xf - mu), axis=-1, keepdims=True)
    y = (xf - mu) * lax.rsqrt(var + LN_EPS)
    return (y * g.astype(jnp.float32) + b.astype(jnp.float32)).astype(x.dtype)


def causal_pool_minus_self(u, window):
    T = u.shape[1]
    uf = u.astype(jnp.float32)
    cs = jnp.cumsum(uf, axis=1)
    lag = jnp.pad(cs, ((0, 0), (window, 0), (0, 0)))[:, :T]
    cnt = jnp.minimum(jnp.arange(1, T + 1), window).astype(jnp.float32)
    return ((cs - lag) / cnt[None, :, None] - uf).astype(u.dtype)


def dsa_attention(q, k, v, qi, ki, wi):
    B, T, H, dh = q.shape
    topk = min(TOPK_MAX, T // 4)
    n_blocks = T // Q_BLOCK
    attn_scale = dh ** -0.5
    idx_scale = IDX_HEAD_DIM ** -0.5
    key_pos = jnp.arange(T)

    def block(i):
        start = i * Q_BLOCK
        q_b = lax.dynamic_slice_in_dim(q, start, Q_BLOCK, axis=1)
        qi_b = lax.dynamic_slice_in_dim(qi, start, Q_BLOCK, axis=1)
        wi_b = lax.dynamic_slice_in_dim(wi, start, Q_BLOCK, axis=1)
        pos = start + jnp.arange(Q_BLOCK)
        s = jnp.einsum('bqhd,bsd->bqhs', qi_b, ki).astype(jnp.float32) * idx_scale
        score = jnp.einsum('bqh,bqhs->bqs', wi_b.astype(jnp.float32), jax.nn.relu(s))
        causal = key_pos[None, :] <= pos[:, None]
        score = jnp.where(causal[None], score, -jnp.inf)
        _, idx = lax.top_k(score, topk)
        valid = idx <= pos[None, :, None]
        k_sel = jax.vmap(lambda kb, ib: kb[ib])(k, idx)
        v_sel = jax.vmap(lambda vb, ib: vb[ib])(v, idx)
        logits = jnp.einsum('bqhd,bqkhd->bhqk', q_b, k_sel).astype(jnp.float32) * attn_scale
        logits = jnp.where(valid[:, None], logits, -jnp.inf)
        p = jax.nn.softmax(logits, axis=-1).astype(v.dtype)
        return jnp.einsum('bhqk,bqkhd->bqhd', p, v_sel)

    out = lax.map(block, jnp.arange(n_blocks))
    return out.transpose(1, 0, 2, 3, 4).reshape(B, T, H, dh)


def mixer_pool_attn(x, w_in, idx_k_ln_g, idx_k_ln_b, pool_w, pool_scale, w_out):
    B, T, _ = x.shape
    h = jnp.einsum('btd,de->bte', x, w_in)
    offsets = np.cumsum(AB_SIZES)[:-1].tolist()
    u, q, k, v, qi, ki, wi = jnp.split(h, offsets, axis=-1)
    ug = u.reshape(B, T, len(POOL_WINDOWS), POOL_GROUP)
    pooled = jnp.stack([causal_pool_minus_self(ug[:, :, g], w)
                        for g, w in enumerate(POOL_WINDOWS)], axis=2)
    a = jnp.einsum('btgc,gce->btge', pooled, pool_w).reshape(B, T, POOL_WIDTH) * pool_scale
    q = q.reshape(B, T, ATTN_HEADS, ATTN_HEAD_DIM)
    k = k.reshape(B, T, ATTN_HEADS, ATTN_HEAD_DIM)
    v = v.reshape(B, T, ATTN_HEADS, ATTN_HEAD_DIM)
    qi = qi.reshape(B, T, IDX_HEADS, IDX_HEAD_DIM)
    ki = layer_norm(ki, idx_k_ln_g, idx_k_ln_b)
    wi = wi * (IDX_HEADS ** -0.5)
    o = dsa_attention(q, k, v, qi, ki, wi).reshape(B, T, ATTN_WIDTH)
    return jnp.einsum('bte,ed->btd', jnp.concatenate([a, o], axis=-1), w_out)


def mixer_short_conv(x, w_in, conv_w, w_out):
    T = x.shape[1]
    h = jnp.einsum('btd,de->bte', x, w_in)
    gate_b, gate_c, hv = jnp.split(h, 3, axis=-1)
    u = gate_c * hv
    up = jnp.pad(u, ((0, 0), (CONV_WIDTH - 1, 0), (0, 0)))
    z = sum(conv_w[j] * up[:, j:j + T] for j in range(CONV_WIDTH))
    return jnp.einsum('btd,de->bte', gate_b * z, w_out)


def hier_moe(x, w_group, b_group, w_expert, b_expert, w_gate, w_up, w_down):
    B, T, D = x.shape
    xt = x.reshape(B * T, D)
    n = xt.shape[0]
    g_logits = (xt @ w_group + b_group).astype(jnp.float32)
    g_prob = jax.nn.softmax(g_logits, axis=-1)
    g_idx = jnp.argmax(g_logits, axis=-1)
    g_w = jnp.take_along_axis(g_prob, g_idx[:, None], axis=-1)
    e_logits = (xt @ w_expert + b_expert).astype(jnp.float32).reshape(n, N_GROUPS, EXPERTS_PER_GROUP)
    e_logits = jnp.take_along_axis(e_logits, g_idx[:, None, None], axis=1)[:, 0]
    e_prob = jax.nn.softmax(e_logits, axis=-1)
    top_p, top_i = lax.top_k(e_prob, EXPERT_TOP_K)
    top_p = top_p / jnp.sum(top_p, axis=-1, keepdims=True)
    gate_e = jnp.sum(jax.nn.one_hot(top_i, EXPERTS_PER_GROUP, dtype=jnp.float32) * top_p[..., None], axis=1)
    gate_g = jax.nn.one_hot(g_idx, N_GROUPS, dtype=jnp.float32) * g_w
    combine = (gate_g[:, :, None] * gate_e[:, None, :]).reshape(n, N_EXPERTS).astype(x.dtype)
    y = jnp.zeros_like(xt)
    for e in range(N_EXPERTS):
        hdn = jax.nn.silu(xt @ w_gate[e]) * (xt @ w_up[e])
        y = y + combine[:, e:e + 1] * (hdn @ w_down[e])
    return y.reshape(B, T, D)


def setup_inputs(seed: int = 0) -> dict:
    key = jax.random.key(seed)
    ks = jax.random.split(key, 24)
    f32 = jnp.float32
    D = D_MODEL

    def nrm(k, shape, scale):
        return jax.random.normal(k, shape, f32) * scale

    return {
        "x": nrm(ks[0], (BATCH, SEQ, D), 1.0),
        "ab_w_in": nrm(ks[1], (N_EVEN, D, AB_IN), D ** -0.5),
        "ab_idx_k_ln_g": 1.0 + nrm(ks[2], (N_EVEN, IDX_HEAD_DIM), 0.02),
        "ab_idx_k_ln_b": nrm(ks[3], (N_EVEN, IDX_HEAD_DIM), 0.02),
        "ab_pool_w": nrm(ks[4], (N_EVEN, len(POOL_WINDOWS), POOL_GROUP, POOL_GROUP), POOL_GROUP ** -0.5),
        "ab_pool_scale": 1.0 + nrm(ks[5], (N_EVEN, POOL_WIDTH), 0.1),
        "ab_w_out": nrm(ks[6], (N_EVEN, POOL_WIDTH + ATTN_WIDTH, D), (POOL_WIDTH + ATTN_WIDTH) ** -0.5 * BETA),
        "c_w_in": nrm(ks[7], (N_ODD, D, 3 * D), D ** -0.5),
        "c_conv_w": nrm(ks[8], (N_ODD, CONV_WIDTH, D), CONV_WIDTH ** -0.5),
        "c_w_out": nrm(ks[9], (N_ODD, D, D), D ** -0.5 * BETA),
        "ln_mix_g": 1.0 + nrm(ks[10], (DEPTH, D), 0.02),
        "ln_mix_b": nrm(ks[11], (DEPTH, D), 0.02),
        "ln_ffn_g": 1.0 + nrm(ks[12], (DEPTH, D), 0.02),
        "ln_ffn_b": nrm(ks[13], (DEPTH, D), 0.02),
        "moe_w_group": nrm(ks[14], (DEPTH, D, N_GROUPS), D ** -0.5),
        "moe_b_group": nrm(ks[15], (DEPTH, N_GROUPS), 0.01),
        "moe_w_expert": nrm(ks[16], (DEPTH, D, N_EXPERTS), D ** -0.5),
        "moe_b_expert": nrm(ks[17], (DEPTH, N_EXPERTS), 0.01),
        "moe_w_gate": nrm(ks[18], (DEPTH, N_EXPERTS, D, D_FF_EXPERT), D ** -0.5),
        "moe_w_up": nrm(ks[19], (DEPTH, N_EXPERTS, D, D_FF_EXPERT), D ** -0.5),
        "moe_w_down": nrm(ks[20], (DEPTH, N_EXPERTS, D_FF_EXPERT, D), D_FF_EXPERT ** -0.5 * BETA),
    }


def reference(x, ab_w_in, ab_idx_k_ln_g, ab_idx_k_ln_b, ab_pool_w, ab_pool_scale, ab_w_out,
              c_w_in, c_conv_w, c_w_out, ln_mix_g, ln_mix_b, ln_ffn_g, ln_ffn_b,
              moe_w_group, moe_b_group, moe_w_expert, moe_b_expert,
              moe_w_gate, moe_w_up, moe_w_down):
    for layer in range(DEPTH):
        i = layer // 2
        if layer % 2 == 0:
            m = mixer_pool_attn(x, ab_w_in[i], ab_idx_k_ln_g[i], ab_idx_k_ln_b[i],
                                ab_pool_w[i], ab_pool_scale[i], ab_w_out[i])
        else:
            m = mixer_short_conv(x, c_w_in[i], c_conv_w[i], c_w_out[i])
        x = layer_norm(ALPHA * x + m, ln_mix_g[layer], ln_mix_b[layer])
        f = hier_moe(x, moe_w_group[layer], moe_b_group[layer], moe_w_expert[layer],
                     moe_b_expert[layer], moe_w_gate[layer], moe_w_up[layer], moe_w_down[layer])
        x = layer_norm(ALPHA * x + f, ln_ffn_g[layer], ln_ffn_b[layer])
    return x
```

```python
import functools

import jax
import jax.numpy as jnp
from jax import lax
from jax.experimental import pallas as pl
from jax.experimental.pallas import tpu as pltpu

f32 = jnp.float32
bf16 = jnp.bfloat16
i32 = jnp.int32

D_MODEL = 2048
POOL_WINDOWS = (2, 4, 8, 16)
POOL_WIDTH = 1024
POOL_GROUP = 256
ATTN_HEADS = 8
ATTN_HEAD_DIM = 128
ATTN_WIDTH = 1024
IDX_HEADS = 16
IDX_HEAD_DIM = 64
TOPK_MAX = 256
CONV_WIDTH = 3
N_GROUPS = 4
EXPERTS_PER_GROUP = 8
N_EXPERTS = 32
D_FF_EXPERT = 512
DEPTH = 2
ALPHA = (2 * DEPTH) ** 0.25
LN_EPS = 1e-5
AB_MAIN = 5 * 1024

VMEM_LIMIT = 56 * 1024 * 1024
MM_TM, MM_TN = 512, 1024
ROW_TM = 256
IDX_TQ, IDX_TK = 128, 512
IDX_KS = 256
ATT_TQ = 256
MOE_TM = 256
NEG_BIG = -1e30
INT_MIN = -(2 ** 31)


def _cparams(sem):
    return pltpu.CompilerParams(dimension_semantics=sem, vmem_limit_bytes=VMEM_LIMIT)


def _layer_norm(z, g, b):
    mu = jnp.mean(z, axis=-1, keepdims=True)
    zc = z - mu
    var = jnp.mean(zc * zc, axis=-1, keepdims=True)
    return zc * lax.rsqrt(var + LN_EPS) * g + b


def _cast_body(x_ref, o_ref):
    o_ref[...] = x_ref[...].astype(o_ref.dtype)


def cast_bf16(x, tm=512):
    m, n = x.shape
    return pl.pallas_call(
        _cast_body,
        grid=(m // tm,),
        in_specs=[pl.BlockSpec((tm, n), lambda i: (i, 0))],
        out_specs=pl.BlockSpec((tm, n), lambda i: (i, 0)),
        out_shape=jax.ShapeDtypeStruct((m, n), bf16),
        compiler_params=_cparams(("parallel",)),
    )(x)


def _mm_body(x_ref, w_ref, o_ref):
    o_ref[...] = jnp.dot(x_ref[...], w_ref[...], preferred_element_type=f32).astype(o_ref.dtype)


def matmul_bf16(x, w, n_cols):
    m, k = x.shape
    return pl.pallas_call(
        _mm_body,
        grid=(n_cols // MM_TN, m // MM_TM),
        in_specs=[pl.BlockSpec((MM_TM, k), lambda j, i: (i, 0)),
                  pl.BlockSpec((k, MM_TN), lambda j, i: (0, j))],
        out_specs=pl.BlockSpec((MM_TM, MM_TN), lambda j, i: (i, j)),
        out_shape=jax.ShapeDtypeStruct((m, n_cols), bf16),
        compiler_params=_cparams(("parallel", "parallel")),
    )(x, w)


def _idx_proj_body(x_ref, wk_ref, ww_ref, g_ref, b_ref, ki_ref, wi_ref, *, wi_scale):
    x = x_ref[...]
    hk = jnp.dot(x, wk_ref[...].astype(bf16), preferred_element_type=f32)
    hw = jnp.dot(x, ww_ref[...].astype(bf16), preferred_element_type=f32)
    ki_ref[...] = _layer_norm(hk, g_ref[...], b_ref[...]).astype(ki_ref.dtype)
    wi_ref[...] = hw * wi_scale


def idx_proj(xb, w_ki, w_wi, g, b):
    m, k = xb.shape
    tm = MM_TM
    wi_scale = (IDX_HEADS ** -0.5) * (IDX_HEAD_DIM ** -0.5)
    return pl.pallas_call(
        functools.partial(_idx_proj_body, wi_scale=wi_scale),
        grid=(m // tm,),
        in_specs=[pl.BlockSpec((tm, k), lambda i: (i, 0)),
                  pl.BlockSpec((k, IDX_HEAD_DIM), lambda i: (0, 0)),
                  pl.BlockSpec((k, IDX_HEADS), lambda i: (0, 0)),
                  pl.BlockSpec((1, IDX_HEAD_DIM), lambda i: (0, 0)),
                  pl.BlockSpec((1, IDX_HEAD_DIM), lambda i: (0, 0))],
        out_specs=[pl.BlockSpec((tm, IDX_HEAD_DIM), lambda i: (i, 0)),
                   pl.BlockSpec((tm, IDX_HEADS), lambda i: (i, 0))],
        out_shape=[jax.ShapeDtypeStruct((m, IDX_HEAD_DIM), bf16),
                   jax.ShapeDtypeStruct((m, IDX_HEADS), f32)],
        compiler_params=_cparams(("parallel",)),
    )(xb, w_ki, w_wi, g.reshape(1, -1), b.reshape(1, -1))


def _pool_body(up_ref, uc_ref, pw_ref, ps_ref, o_ref, *, tiles_per_seq):
    tm = uc_ref.shape[0]
    i = pl.program_id(0)
    t0 = (i % tiles_per_seq) * tm
    row = lax.broadcasted_iota(i32, (tm, 2 * tm), 0)
    col = lax.broadcasted_iota(i32, (tm, 2 * tm), 1) - tm
    lo_ok = col + t0 >= 0
    pos = t0 + lax.broadcasted_iota(i32, (tm, 1), 0)
    for g, win in enumerate(POOL_WINDOWS):
        cs = slice(g * POOL_GROUP, (g + 1) * POOL_GROUP)
        band = jnp.where((col <= row) & (col > row - win) & lo_ok, 1.0, 0.0).astype(bf16)
        uc = uc_ref[:, cs]
        ucat = jnp.concatenate([up_ref[:, cs], uc], axis=0)
        ssum = jnp.dot(band, ucat, preferred_element_type=f32)
        cnt = jnp.minimum(pos + 1, win).astype(f32)
        pooled = ssum / cnt - uc.astype(f32)
        a = jnp.dot(pooled.astype(bf16), pw_ref[g].astype(bf16), preferred_element_type=f32)
        o_ref[:, cs] = (a * ps_ref[:, cs]).astype(o_ref.dtype)


def pool_mixer(h, pool_w, pool_scale, seq_len):
    m = h.shape[0]
    tm = ROW_TM
    return pl.pallas_call(
        functools.partial(_pool_body, tiles_per_seq=seq_len // tm),
        grid=(m // tm,),
        in_specs=[pl.BlockSpec((tm, POOL_WIDTH), lambda i: (jnp.maximum(i - 1, 0), 0)),
                  pl.BlockSpec((tm, POOL_WIDTH), lambda i: (i, 0)),
                  pl.BlockSpec((len(POOL_WINDOWS), POOL_GROUP, POOL_GROUP), lambda i: (0, 0, 0)),
                  pl.BlockSpec((1, POOL_WIDTH), lambda i: (0, 0))],
        out_specs=pl.BlockSpec((tm, POOL_WIDTH), lambda i: (i, 0)),
        out_shape=jax.ShapeDtypeStruct((m, POOL_WIDTH), bf16),
        compiler_params=_cparams(("parallel",)),
    )(h, h, pool_w, pool_scale.reshape(1, -1))


def _sortable(x):
    bits = pltpu.bitcast(x, i32)
    return bits ^ ((bits >> 31) & 0x7FFFFFFF)


def _dsa_index_body(qi_ref, ki_ref, wi_ref, o_ref, key_ref, *, topk):
    tq, tk, ks = IDX_TQ, IDX_TK, IDX_KS
    n_chunks = key_ref.shape[0]
    q0 = pl.program_id(1) * tq
    n_live = (q0 + tq + tk - 1) // tk
    w = wi_ref[...]
    qpos = q0 + lax.broadcasted_iota(i32, (tq, ks), 0)
    kcol = lax.broadcasted_iota(i32, (tq, ks), 1)

    def score_chunk(c, carry):
        k0 = pl.multiple_of(c * tk, tk)
        for s in range(tk // ks):
            kic = ki_ref[pl.ds(k0 + s * ks, ks), :]
            acc = jnp.zeros((tq, ks), f32)
            for h in range(IDX_HEADS):
                qh = qi_ref[:, h * IDX_HEAD_DIM:(h + 1) * IDX_HEAD_DIM]
                sc = lax.dot_general(qh, kic, (((1,), (1,)), ((), ())), preferred_element_type=f32)
                acc = acc + w[:, h:h + 1] * jnp.maximum(sc, 0.0)
            acc = jnp.where(kcol + (k0 + s * ks) <= qpos, acc, -jnp.inf)
            key_ref[c, :, s * ks:(s + 1) * ks] = _sortable(acc)
        return carry

    lax.fori_loop(0, n_live, score_chunk, 0)

    def bit_step(b, cand):
        test = cand | jnp.left_shift(jnp.int32(1), 31 - b)
        test_s = test ^ INT_MIN

        def count_chunk(c, part):
            kk = key_ref[c]
            for j in range(tk // 128):
                part = part + jnp.where(kk[:, j * 128:(j + 1) * 128] >= test_s, 1, 0)
            return part

        part = lax.fori_loop(0, n_live, count_chunk, jnp.zeros((tq, 128), i32))
        cnt = jnp.sum(part, axis=1, keepdims=True)
        return jnp.where(cnt >= topk, test, cand)

    cand = lax.fori_loop(0, 32, bit_step, jnp.zeros((tq, 1), i32))
    neg_inf_key = -(2 ** 31) + 0x007FFFFF
    thr = jnp.maximum(cand ^ INT_MIN, neg_inf_key + 1)

    def bias_chunk(c, carry):
        o_ref[0, c] = jnp.where(key_ref[c] >= thr, 0.0, NEG_BIG).astype(o_ref.dtype)
        return carry

    lax.fori_loop(0, n_live, bias_chunk, 0)

    def dead_chunk(c, carry):
        o_ref[0, c] = jnp.full((tq, tk), NEG_BIG, o_ref.dtype)
        return carry

    lax.fori_loop(n_live, n_chunks, dead_chunk, 0)


def dsa_index(h, ki, wi, batch, seq_len):
    tq, tk = IDX_TQ, IDX_TK
    nq, nc = seq_len // tq, seq_len // tk
    topk = min(TOPK_MAX, seq_len // 4)
    qi_col = 4
    return pl.pallas_call(
        functools.partial(_dsa_index_body, topk=topk),
        grid=(batch, nq),
        in_specs=[pl.BlockSpec((tq, IDX_HEADS * IDX_HEAD_DIM), lambda b, i: (b * nq + i, qi_col)),
                  pl.BlockSpec((seq_len, IDX_HEAD_DIM), lambda b, i: (b, 0)),
                  pl.BlockSpec((tq, IDX_HEADS), lambda b, i: (b * nq + i, 0))],
        out_specs=pl.BlockSpec((1, nc, tq, tk), lambda b, i: (b * nq + i, 0, 0, 0)),
        out_shape=jax.ShapeDtypeStruct((batch * nq, nc, tq, tk), bf16),
        scratch_shapes=[pltpu.VMEM((nc, tq, tk), i32)],
        compiler_params=_cparams(("parallel", "parallel")),
    )(h, ki, wi)


def _dsa_attn_body(q_ref, k_ref, v_ref, bias_ref, o_ref, m_ref, l_ref, acc_ref):
    tq, tk = ATT_TQ, IDX_TK
    i, c = pl.program_id(1), pl.program_id(2)
    last = ((i + 1) * tq - 1) // tk
    scale = ATTN_HEAD_DIM ** -0.5

    @pl.when(c == 0)
    def _():
        m_ref[...] = jnp.full(m_ref.shape, NEG_BIG, f32)
        l_ref[...] = jnp.zeros(l_ref.shape, f32)
        acc_ref[...] = jnp.zeros(acc_ref.shape, f32)

    @pl.when(c <= last)
    def _():
        bias = bias_ref[...].reshape(tq, tk).astype(f32)
        for h in range(ATTN_HEADS):
            hs = slice(h * ATTN_HEAD_DIM, (h + 1) * ATTN_HEAD_DIM)
            s = lax.dot_general(q_ref[:, hs], k_ref[:, hs], (((1,), (1,)), ((), ())),
                                preferred_element_type=f32) * scale + bias
            m_old = m_ref[h]
            m_new = jnp.maximum(m_old, jnp.max(s, axis=1, keepdims=True))
            p = jnp.exp(s - m_new)
            a = jnp.exp(m_old - m_new)
            l_ref[h] = a * l_ref[h] + jnp.sum(p, axis=1, keepdims=True)
            acc_ref[:, hs] = a * acc_ref[:, hs] + jnp.dot(p.astype(bf16), v_ref[:, hs], preferred_element_type=f32)
            m_ref[h] = m_new

    @pl.when(c == pl.num_programs(2) - 1)
    def _():
        for h in range(ATTN_HEADS):
            hs = slice(h * ATTN_HEAD_DIM, (h + 1) * ATTN_HEAD_DIM)
            o_ref[:, hs] = (acc_ref[:, hs] / l_ref[h]).astype(o_ref.dtype)


def dsa_attention(h, bias, batch, seq_len):
    tq, tk = ATT_TQ, IDX_TK
    nq, nc = seq_len // tq, seq_len // tk
    sub = tq // IDX_TQ

    def kc(i, c):
        return jnp.minimum(c, ((i + 1) * tq - 1) // tk)

    return pl.pallas_call(
        _dsa_attn_body,
        grid=(batch, nq, nc),
        in_specs=[pl.BlockSpec((tq, ATTN_WIDTH), lambda b, i, c: (b * nq + i, 1)),
                  pl.BlockSpec((tk, ATTN_WIDTH), lambda b, i, c: (b * nc + kc(i, c), 2)),
                  pl.BlockSpec((tk, ATTN_WIDTH), lambda b, i, c: (b * nc + kc(i, c), 3)),
                  pl.BlockSpec((sub, 1, IDX_TQ, tk), lambda b, i, c: (b * nq + i, kc(i, c), 0, 0))],
        out_specs=pl.BlockSpec((tq, ATTN_WIDTH), lambda b, i, c: (b * nq + i, 0)),
        out_shape=jax.ShapeDtypeStruct((batch * seq_len, ATTN_WIDTH), bf16),
        scratch_shapes=[pltpu.VMEM((ATTN_HEADS, tq, 1), f32),
                        pltpu.VMEM((ATTN_HEADS, tq, 1), f32),
                        pltpu.VMEM((tq, ATTN_WIDTH), f32)],
        compiler_params=_cparams(("parallel", "parallel", "arbitrary")),
    )(h, h, h, bias)


def _out_proj_ln_body(a_ref, o_ref, w_ref, x_ref, g_ref, b_ref, y_ref, yb_ref):
    acc = jnp.dot(a_ref[...], w_ref[:POOL_WIDTH, :], preferred_element_type=f32)
    acc = acc + jnp.dot(o_ref[...], w_ref[POOL_WIDTH:, :], preferred_element_type=f32)
    y = _layer_norm(ALPHA * x_ref[...] + acc, g_ref[...], b_ref[...])
    y_ref[...] = y
    yb_ref[...] = y.astype(yb_ref.dtype)


def out_proj_ln(a, o, w_out_b, x, g, b):
    m = x.shape[0]
    tm = ROW_TM
    return pl.pallas_call(
        _out_proj_ln_body,
        grid=(m // tm,),
        in_specs=[pl.BlockSpec((tm, POOL_WIDTH), lambda i: (i, 0)),
                  pl.BlockSpec((tm, ATTN_WIDTH), lambda i: (i, 0)),
                  pl.BlockSpec((POOL_WIDTH + ATTN_WIDTH, D_MODEL), lambda i: (0, 0)),
                  pl.BlockSpec((tm, D_MODEL), lambda i: (i, 0)),
                  pl.BlockSpec((1, D_MODEL), lambda i: (0, 0)),
                  pl.BlockSpec((1, D_MODEL), lambda i: (0, 0))],
        out_specs=[pl.BlockSpec((tm, D_MODEL), lambda i: (i, 0)),
                   pl.BlockSpec((tm, D_MODEL), lambda i: (i, 0))],
        out_shape=[jax.ShapeDtypeStruct((m, D_MODEL), f32),
                   jax.ShapeDtypeStruct((m, D_MODEL), bf16)],
        compiler_params=_cparams(("parallel",)),
    )(a, o, w_out_b, x, g.reshape(1, -1), b.reshape(1, -1))


def _conv_out_ln_body(gb_ref, gc_ref, hv_ref, gch_ref, hvh_ref, cw_ref, w_ref, x_ref, g_ref, b_ref, y_ref, yb_ref,
                      *, tiles_per_seq):
    tm = gb_ref.shape[0]
    first = (pl.program_id(0) % tiles_per_seq) == 0
    u = gc_ref[...].astype(f32) * hv_ref[...].astype(f32)
    uh = gch_ref[...].astype(f32) * hvh_ref[...].astype(f32)
    uh = jnp.where(first, 0.0, uh)
    hr = uh.shape[0]
    row = lax.broadcasted_iota(i32, (tm, 1), 0)
    u1 = jnp.where(row == 0, uh[hr - 1:hr, :], pltpu.roll(u, 1, 0))
    u2 = jnp.where(row == 0, uh[hr - 2:hr - 1, :],
                   jnp.where(row == 1, uh[hr - 1:hr, :], pltpu.roll(u, 2, 0)))
    z = cw_ref[0:1, :] * u2 + cw_ref[1:2, :] * u1 + cw_ref[2:3, :] * u
    gz = (gb_ref[...].astype(f32) * z).astype(bf16)
    acc = jnp.dot(gz, w_ref[...], preferred_element_type=f32)
    y = _layer_norm(ALPHA * x_ref[...] + acc, g_ref[...], b_ref[...])
    y_ref[...] = y
    yb_ref[...] = y.astype(yb_ref.dtype)


def conv_out_ln(h1, conv_w, w_out_b, x, g, b, seq_len):
    m = x.shape[0]
    tm = ROW_TM
    halo = 16
    hb = tm // halo

    def halo_idx(col):
        return lambda i: (jnp.maximum(i * hb - 1, 0), col)

    return pl.pallas_call(
        functools.partial(_conv_out_ln_body, tiles_per_seq=seq_len // tm),
        grid=(m // tm,),
        in_specs=[pl.BlockSpec((tm, D_MODEL), lambda i: (i, 0)),
                  pl.BlockSpec((tm, D_MODEL), lambda i: (i, 1)),
                  pl.BlockSpec((tm, D_MODEL), lambda i: (i, 2)),
                  pl.BlockSpec((halo, D_MODEL), halo_idx(1)),
                  pl.BlockSpec((halo, D_MODEL), halo_idx(2)),
                  pl.BlockSpec((CONV_WIDTH, D_MODEL), lambda i: (0, 0)),
                  pl.BlockSpec((D_MODEL, D_MODEL), lambda i: (0, 0)),
                  pl.BlockSpec((tm, D_MODEL), lambda i: (i, 0)),
                  pl.BlockSpec((1, D_MODEL), lambda i: (0, 0)),
                  pl.BlockSpec((1, D_MODEL), lambda i: (0, 0))],
        out_specs=[pl.BlockSpec((tm, D_MODEL), lambda i: (i, 0)),
                   pl.BlockSpec((tm, D_MODEL), lambda i: (i, 0))],
        out_shape=[jax.ShapeDtypeStruct((m, D_MODEL), f32),
                   jax.ShapeDtypeStruct((m, D_MODEL), bf16)],
        compiler_params=_cparams(("parallel",)),
    )(h1, h1, h1, h1, h1, conv_w, w_out_b, x, g.reshape(1, -1), b.reshape(1, -1))


def _split_bf16(x):
    hi = x.astype(bf16)
    lo = (x - hi.astype(f32)).astype(bf16)
    return hi, lo


def _router_body(x_ref, w_ref, b_ref, o_ref):
    xh, xl = _split_bf16(x_ref[...])
    wh, wl = _split_bf16(w_ref[...])
    lg = (jnp.dot(xh, wh, preferred_element_type=f32) + jnp.dot(xh, wl, preferred_element_type=f32)
          + jnp.dot(xl, wh, preferred_element_type=f32)) + b_ref[...]
    tm, nl = lg.shape
    lane = lax.broadcasted_iota(i32, (tm, nl), 1)
    neg = -jnp.inf
    is_g = lane < N_GROUPS
    gl = jnp.where(is_g, lg, neg)
    gmax = jnp.max(gl, axis=1, keepdims=True)
    gidx = jnp.min(jnp.where(gl == gmax, lane, nl), axis=1, keepdims=True)
    gw = 1.0 / jnp.sum(jnp.where(is_g, jnp.exp(lg - gmax), 0.0), axis=1, keepdims=True)
    e_lo = N_GROUPS + gidx * EXPERTS_PER_GROUP
    in_grp = (lane >= e_lo) & (lane < e_lo + EXPERTS_PER_GROUP)
    el = jnp.where(in_grp, lg, neg)
    l0 = jnp.max(el, axis=1, keepdims=True)
    i0 = jnp.min(jnp.where(el == l0, lane, nl), axis=1, keepdims=True)
    el1 = jnp.where(lane == i0, neg, el)
    l1 = jnp.max(el1, axis=1, keepdims=True)
    i1 = jnp.min(jnp.where(el1 == l1, lane, nl), axis=1, keepdims=True)
    p1 = jnp.exp(l1 - l0)
    c0 = gw / (1.0 + p1)
    c1 = gw * p1 / (1.0 + p1)
    out = jnp.where(lane == 0, (i0 - N_GROUPS).astype(f32),
                    jnp.where(lane == 1, (i1 - N_GROUPS).astype(f32),
                              jnp.where(lane == 2, c0, jnp.where(lane == 3, c1, 0.0))))
    o_ref[...] = out


def moe_router(x, w_group, b_group, w_expert, b_expert):
    m = x.shape[0]
    tm = ROW_TM
    nl = 128
    w = jnp.zeros((D_MODEL, nl), f32).at[:, :N_GROUPS].set(w_group).at[:, N_GROUPS:N_GROUPS + N_EXPERTS].set(w_expert)
    bias = jnp.zeros((1, nl), f32).at[0, :N_GROUPS].set(b_group).at[0, N_GROUPS:N_GROUPS + N_EXPERTS].set(b_expert)
    return pl.pallas_call(
        _router_body,
        grid=(m // tm,),
        in_specs=[pl.BlockSpec((tm, D_MODEL), lambda i: (i, 0)),
                  pl.BlockSpec((D_MODEL, nl), lambda i: (0, 0)),
                  pl.BlockSpec((1, nl), lambda i: (0, 0))],
        out_specs=pl.BlockSpec((tm, nl), lambda i: (i, 0)),
        out_shape=jax.ShapeDtypeStruct((m, nl), f32),
        compiler_params=_cparams(("parallel",)),
    )(x, w, bias)


def _route_plan(route, tm, n_tiles):
    n = route.shape[0]
    e_flat = jnp.concatenate([route[:, 0], route[:, 1]]).astype(i32)
    w_flat = jnp.concatenate([route[:, 2], route[:, 3]])
    tok = jnp.concatenate([jnp.arange(n, dtype=i32), jnp.arange(n, dtype=i32)])
    onehot = (e_flat[:, None] == jnp.arange(N_EXPERTS, dtype=i32)[None, :]).astype(i32)
    csum = jnp.cumsum(onehot, axis=0)
    rank = jnp.sum(csum * onehot, axis=1) - 1
    counts = csum[-1]
    tiles_per = (counts + tm - 1) // tm
    tile_end = jnp.cumsum(tiles_per)
    tile_start = tile_end - tiles_per
    pos = tile_start[e_flat] * tm + rank
    p_total = n_tiles * tm
    src_tok = jnp.zeros((p_total,), i32).at[pos].set(tok)
    row_w = jnp.zeros((p_total,), f32).at[pos].set(w_flat)
    tile_ids = jnp.arange(n_tiles, dtype=i32)
    n_valid = tile_end[-1]
    valid = tile_ids < n_valid
    tile_e = jnp.sum((tile_ids[:, None] >= tile_end[None, :]).astype(i32), axis=1)
    last_e = jnp.sum((n_valid - 1 >= tile_end).astype(i32))
    tile_e = jnp.where(valid, tile_e, last_e).astype(i32)
    prev_e = jnp.concatenate([jnp.full((1,), -1, i32), tile_e[:-1]])
    first = ((tile_e != prev_e) & valid).astype(i32)
    return src_tok, row_w.reshape(-1, 1), tile_e, first, valid.astype(i32), pos[:n], pos[n:]


def _row_copy(src_hbm, dst_ref, sem, src_row, dst_row):
    return pltpu.make_async_copy(src_hbm.at[pl.ds(src_row, 1), :], dst_ref.at[pl.ds(dst_row, 1), :], sem)


def _moe_ffn_body(src_ref, te_ref, first_ref, valid_ref, x_hbm, rw_ref, wg_ref, wu_ref, wd_ref, y_ref,
                  xg_ref, sem_ref, wgb_ref, wub_ref, wdb_ref):
    tm = MOE_TM
    t = pl.program_id(0)
    nt = pl.num_programs(0)
    slot = t % 2

    def gather(tile, sl, start):
        def body(r, carry):
            cp = _row_copy(x_hbm, xg_ref.at[sl], sem_ref.at[sl], src_ref[tile * tm + r], r)
            if start:
                cp.start()
            else:
                cp.wait()
            return carry
        lax.fori_loop(0, tm, body, 0)

    @pl.when((t == 0) & (valid_ref[0] == 1))
    def _():
        gather(0, 0, True)

    nxt = jnp.minimum(t + 1, nt - 1)

    @pl.when((t + 1 < nt) & (valid_ref[nxt] == 1))
    def _():
        gather(nxt, 1 - slot, True)

    @pl.when(first_ref[t] == 1)
    def _():
        wgb_ref[...] = wg_ref[0].astype(bf16)
        wub_ref[...] = wu_ref[0].astype(bf16)
        wdb_ref[...] = wd_ref[0].astype(bf16)

    @pl.when(valid_ref[t] == 1)
    def _():
        gather(t, slot, False)
        xb = xg_ref[slot].astype(bf16)
        gate = jnp.dot(xb, wgb_ref[...], preferred_element_type=f32)
        up = jnp.dot(xb, wub_ref[...], preferred_element_type=f32)
        hdn = (gate * jax.nn.sigmoid(gate) * up).astype(bf16)
        y = jnp.dot(hdn, wdb_ref[...], preferred_element_type=f32)
        y_ref[...] = y * rw_ref[...]

    @pl.when(valid_ref[t] == 0)
    def _():
        y_ref[...] = jnp.zeros(y_ref.shape, y_ref.dtype)


def moe_ffn(x, plan, w_gate, w_up, w_down, n_tiles):
    src_tok, row_w, tile_e, first, valid = plan
    tm = MOE_TM
    grid_spec = pltpu.PrefetchScalarGridSpec(
        num_scalar_prefetch=4,
        grid=(n_tiles,),
        in_specs=[pl.BlockSpec(memory_space=pl.ANY),
                  pl.BlockSpec((tm, 1), lambda t, s, e, f, v: (t, 0)),
                  pl.BlockSpec((1, D_MODEL, D_FF_EXPERT), lambda t, s, e, f, v: (e[t], 0, 0)),
                  pl.BlockSpec((1, D_MODEL, D_FF_EXPERT), lambda t, s, e, f, v: (e[t], 0, 0)),
                  pl.BlockSpec((1, D_FF_EXPERT, D_MODEL), lambda t, s, e, f, v: (e[t], 0, 0))],
        out_specs=pl.BlockSpec((tm, D_MODEL), lambda t, s, e, f, v: (t, 0)),
        scratch_shapes=[pltpu.VMEM((2, tm, D_MODEL), f32),
                        pltpu.SemaphoreType.DMA((2,)),
                        pltpu.VMEM((D_MODEL, D_FF_EXPERT), bf16),
                        pltpu.VMEM((D_MODEL, D_FF_EXPERT), bf16),
                        pltpu.VMEM((D_FF_EXPERT, D_MODEL), bf16)],
    )
    return pl.pallas_call(
        _moe_ffn_body,
        grid_spec=grid_spec,
        out_shape=jax.ShapeDtypeStruct((n_tiles * tm, D_MODEL), f32),
        compiler_params=_cparams(("arbitrary",)),
    )(src_tok, tile_e, first, valid, x, row_w, w_gate, w_up, w_down)


def _combine_ln_body(p0_ref, p1_ref, y_hbm, x_ref, g_ref, b_ref, o_ref, ob_ref, yg_ref, sem_ref):
    tm = ROW_TM
    t = pl.program_id(0)
    nt = pl.num_programs(0)
    slot = t % 2

    def gather(tile, sl, start):
        def body(r, carry):
            for j, p_ref in enumerate((p0_ref, p1_ref)):
                cp = _row_copy(y_hbm, yg_ref.at[sl, j], sem_ref.at[sl], p_ref[tile * tm + r], r)
                if start:
                    cp.start()
                else:
                    cp.wait()
            return carry
        lax.fori_loop(0, tm, body, 0)

    @pl.when(t == 0)
    def _():
        gather(0, 0, True)

    @pl.when(t + 1 < nt)
    def _():
        gather(t + 1, 1 - slot, True)

    gather(t, slot, False)
    f = yg_ref[slot, 0] + yg_ref[slot, 1]
    y = _layer_norm(ALPHA * x_ref[...] + f, g_ref[...], b_ref[...])
    o_ref[...] = y
    ob_ref[...] = y.astype(ob_ref.dtype)


def combine_ln(y_sorted, pos0, pos1, x, g, b):
    m = x.shape[0]
    tm = ROW_TM
    grid_spec = pltpu.PrefetchScalarGridSpec(
        num_scalar_prefetch=2,
        grid=(m // tm,),
        in_specs=[pl.BlockSpec(memory_space=pl.ANY),
                  pl.BlockSpec((tm, D_MODEL), lambda t, p0, p1: (t, 0)),
                  pl.BlockSpec((1, D_MODEL), lambda t, p0, p1: (0, 0)),
                  pl.BlockSpec((1, D_MODEL), lambda t, p0, p1: (0, 0))],
        out_specs=[pl.BlockSpec((tm, D_MODEL), lambda t, p0, p1: (t, 0)),
                   pl.BlockSpec((tm, D_MODEL), lambda t, p0, p1: (t, 0))],
        scratch_shapes=[pltpu.VMEM((2, 2, tm, D_MODEL), f32),
                        pltpu.SemaphoreType.DMA((2,))],
    )
    return pl.pallas_call(
        _combine_ln_body,
        grid_spec=grid_spec,
        out_shape=[jax.ShapeDtypeStruct((m, D_MODEL), f32),
                   jax.ShapeDtypeStruct((m, D_MODEL), bf16)],
        compiler_params=_cparams(("arbitrary",)),
    )(pos0, pos1, y_sorted, x, g.reshape(1, -1), b.reshape(1, -1))


def hier_moe_ln(x, w_group, b_group, w_expert, b_expert, w_gate, w_up, w_down, g, b):
    n = x.shape[0]
    n_tiles = 2 * n // MOE_TM + N_EXPERTS
    route = moe_router(x, w_group, b_group, w_expert, b_expert)
    src_tok, row_w, tile_e, first, valid, pos0, pos1 = _route_plan(route, MOE_TM, n_tiles)
    y_sorted = moe_ffn(x, (src_tok, row_w, tile_e, first, valid), w_gate, w_up, w_down, n_tiles)
    return combine_ln(y_sorted, pos0, pos1, x, g, b)


def kernel(x, ab_w_in, ab_idx_k_ln_g, ab_idx_k_ln_b, ab_pool_w, ab_pool_scale, ab_w_out, c_w_in, c_conv_w, c_w_out,
           ln_mix_g, ln_mix_b, ln_ffn_g, ln_ffn_b, moe_w_group, moe_b_group, moe_w_expert, moe_b_expert,
           moe_w_gate, moe_w_up, moe_w_down):
    batch, seq_len, d = x.shape
    n = batch * seq_len
    xf = x.reshape(n, d)

    xb = cast_bf16(xf)
    w_in = ab_w_in[0]
    h = matmul_bf16(xb, cast_bf16(w_in), AB_MAIN)
    ki, wi = idx_proj(xb, w_in[:, AB_MAIN:AB_MAIN + IDX_HEAD_DIM], w_in[:, AB_MAIN + IDX_HEAD_DIM:],
                      ab_idx_k_ln_g[0], ab_idx_k_ln_b[0])
    a = pool_mixer(h, ab_pool_w[0], ab_pool_scale[0], seq_len)
    bias = dsa_index(h, ki, wi, batch, seq_len)
    o = dsa_attention(h, bias, batch, seq_len)
    x1, _ = out_proj_ln(a, o, cast_bf16(ab_w_out[0]), xf, ln_mix_g[0], ln_mix_b[0])
    x2, x2b = hier_moe_ln(x1, moe_w_group[0], moe_b_group[0], moe_w_expert[0], moe_b_expert[0],
                          moe_w_gate[0], moe_w_up[0], moe_w_down[0], ln_ffn_g[0], ln_ffn_b[0])

    h1 = matmul_bf16(x2b, cast_bf16(c_w_in[0]), 3 * D_MODEL)
    x3, _ = conv_out_ln(h1, c_conv_w[0], cast_bf16(c_w_out[0]), x2, ln_mix_g[1], ln_mix_b[1], seq_len)
    x4, _ = hier_moe_ln(x3, moe_w_group[1], moe_b_group[1], moe_w_expert[1], moe_b_expert[1],
                        moe_w_gate[1], moe_w_up[1], moe_w_down[1], ln_ffn_g[1], ln_ffn_b[1])
    return x4.reshape(batch, seq_len, d)
```

```python
import functools
import math

import numpy as np
import jax
import jax.numpy as jnp
from jax import lax
from jax.experimental import pallas as pl
from jax.experimental.pallas import tpu as pltpu

f32 = jnp.float32
bf16 = jnp.bfloat16
i32 = jnp.int32

D_MODEL = 2048
POOL_WINDOWS = (2, 4, 8, 16)
POOL_WIDTH = 1024
POOL_GROUP = 256
ATTN_HEADS = 8
ATTN_HEAD_DIM = 128
ATTN_WIDTH = 1024
IDX_HEADS = 16
IDX_HEAD_DIM = 64
TOPK_MAX = 256
CONV_WIDTH = 3
N_GROUPS = 4
EXPERTS_PER_GROUP = 8
N_EXPERTS = 32
D_FF_EXPERT = 512
DEPTH = 2
ALPHA = (2 * DEPTH) ** 0.25
LN_EPS = 1e-5
COL = 1024
V_COL = 3

LANES = 128
VMEM_LIMIT = 56 * 1024 * 1024
MM_TM, MM_TN = 512, 1024
ROW_TM = 256
IDX_TQ, IDX_TK = 256, 512
IDX_KS = 128
ATT_TQ = 256
VT_ROWS = ATTN_HEAD_DIM + 16
MOE_TM = 256
DMA_UNROLL = 8
SLAB = D_MODEL // LANES
NEG_BIG = -1e30
INT_MIN = -(2 ** 31)
LOG2E = math.log2(math.e)


def _cparams(sem):
    return pltpu.CompilerParams(dimension_semantics=sem, vmem_limit_bytes=VMEM_LIMIT)


def _layer_norm(z, g, b):
    mu = jnp.mean(z, axis=-1, keepdims=True)
    zc = z - mu
    var = jnp.mean(zc * zc, axis=-1, keepdims=True)
    return zc * lax.rsqrt(var + LN_EPS) * g + b


def _cast_body(x_ref, o_ref):
    o_ref[...] = x_ref[...].astype(o_ref.dtype)


def cast_bf16(x, tm=512):
    m, n = x.shape
    return pl.pallas_call(
        _cast_body,
        grid=(m // tm,),
        in_specs=[pl.BlockSpec((tm, n), lambda i: (i, 0))],
        out_specs=pl.BlockSpec((tm, n), lambda i: (i, 0)),
        out_shape=jax.ShapeDtypeStruct((m, n), bf16),
        compiler_params=_cparams(("parallel",)),
        name="cast_bf16",
    )(x)


def _mm_body(x_ref, w_ref, o_ref, *, scales):
    acc = jnp.dot(x_ref[...], w_ref[...], preferred_element_type=f32)
    j = pl.program_id(0)
    scale = jnp.float32(1.0)
    for jj, s in enumerate(scales):
        if s != 1.0:
            scale = jnp.where(j == jj, jnp.float32(s), scale)
    o_ref[...] = (acc * scale).astype(o_ref.dtype)


def matmul_bf16(x, w, col_blocks, scales=None):
    m, k = x.shape
    nb = len(col_blocks)
    scales = tuple(scales) if scales is not None else (1.0,) * nb
    skip = [c for c in range(col_blocks[0], col_blocks[-1] + 1) if c not in col_blocks]
    assert len(skip) <= 1 and list(col_blocks) == sorted(col_blocks)
    first = col_blocks[0]

    def w_col(j):
        c = j + first
        return c + (c >= skip[0]).astype(i32) if skip else c

    return pl.pallas_call(
        functools.partial(_mm_body, scales=scales),
        grid=(nb, m // MM_TM),
        in_specs=[pl.BlockSpec((MM_TM, k), lambda j, i: (i, 0)),
                  pl.BlockSpec((k, MM_TN), lambda j, i: (0, w_col(j)))],
        out_specs=pl.BlockSpec((MM_TM, MM_TN), lambda j, i: (i, j)),
        out_shape=jax.ShapeDtypeStruct((m, nb * MM_TN), bf16),
        compiler_params=_cparams(("parallel", "parallel")),
        name="proj_matmul",
    )(x, w)


def _value_t_body(wt_ref, x_ref, o_ref):
    vt = lax.dot_general(wt_ref[...], x_ref[...], (((1,), (1,)), ((), ())), preferred_element_type=f32)
    dh = ATTN_HEAD_DIM
    for h in range(ATTN_HEADS):
        o_ref[h, :dh, :] = vt[h * dh:(h + 1) * dh, :].astype(o_ref.dtype)
        o_ref[h, dh:, :] = jnp.ones((VT_ROWS - dh, o_ref.shape[2]), o_ref.dtype)


def value_t(wt, x):
    c, k = wt.shape
    m = x.shape[0]
    return pl.pallas_call(
        _value_t_body,
        grid=(m // MM_TM,),
        in_specs=[pl.BlockSpec((c, k), lambda i: (0, 0)),
                  pl.BlockSpec((MM_TM, k), lambda i: (i, 0))],
        out_specs=pl.BlockSpec((ATTN_HEADS, VT_ROWS, MM_TM), lambda i: (0, 0, i)),
        out_shape=jax.ShapeDtypeStruct((ATTN_HEADS, VT_ROWS, m), bf16),
        compiler_params=_cparams(("parallel",)),
        name="value_t",
    )(wt, x)


def _idx_proj_body(x_ref, wk_ref, ww_ref, g_ref, b_ref, ki_ref, wi_ref, *, wi_scale):
    x = x_ref[...]
    hk = jnp.dot(x, wk_ref[...].astype(bf16), preferred_element_type=f32)
    hw = jnp.dot(x, ww_ref[...].astype(bf16), preferred_element_type=f32)
    ki_ref[...] = _layer_norm(hk, g_ref[...], b_ref[...]).astype(ki_ref.dtype)
    wi_ref[...] = hw * wi_scale


def idx_proj(xb, w_ki, w_wi, g, b):
    m, k = xb.shape
    tm = MM_TM
    wi_scale = (IDX_HEADS ** -0.5) * (IDX_HEAD_DIM ** -0.5)
    return pl.pallas_call(
        functools.partial(_idx_proj_body, wi_scale=wi_scale),
        grid=(m // tm,),
        in_specs=[pl.BlockSpec((tm, k), lambda i: (i, 0)),
                  pl.BlockSpec((k, IDX_HEAD_DIM), lambda i: (0, 0)),
                  pl.BlockSpec((k, LANES), lambda i: (0, 0)),
                  pl.BlockSpec((1, IDX_HEAD_DIM), lambda i: (0, 0)),
                  pl.BlockSpec((1, IDX_HEAD_DIM), lambda i: (0, 0))],
        out_specs=[pl.BlockSpec((tm, IDX_HEAD_DIM), lambda i: (i, 0)),
                   pl.BlockSpec((tm, LANES), lambda i: (i, 0))],
        out_shape=[jax.ShapeDtypeStruct((m, IDX_HEAD_DIM), bf16),
                   jax.ShapeDtypeStruct((m, LANES), f32)],
        compiler_params=_cparams(("parallel",)),
        name="idx_proj",
    )(xb, w_ki, w_wi, g.reshape(1, -1), b.reshape(1, -1))


def _pool_body(up_ref, uc_ref, pw_ref, ps_ref, o_ref, *, tiles_per_seq):
    tm = uc_ref.shape[0]
    i = pl.program_id(0)
    t0 = (i % tiles_per_seq) * tm
    row = lax.broadcasted_iota(i32, (tm, 2 * tm), 0)
    col = lax.broadcasted_iota(i32, (tm, 2 * tm), 1) - tm
    lo_ok = col + t0 >= 0
    pos = t0 + lax.broadcasted_iota(i32, (tm, 1), 0)
    for g, win in enumerate(POOL_WINDOWS):
        cs = slice(g * POOL_GROUP, (g + 1) * POOL_GROUP)
        band = jnp.where((col <= row) & (col > row - win) & lo_ok, 1.0, 0.0).astype(bf16)
        uc = uc_ref[:, cs]
        ucat = jnp.concatenate([up_ref[:, cs], uc], axis=0)
        ssum = jnp.dot(band, ucat, preferred_element_type=f32)
        cnt = jnp.minimum(pos + 1, win).astype(f32)
        pooled = ssum / cnt - uc.astype(f32)
        a = jnp.dot(pooled.astype(bf16), pw_ref[g].astype(bf16), preferred_element_type=f32)
        o_ref[:, cs] = (a * ps_ref[:, cs]).astype(o_ref.dtype)


def pool_mixer(h, pool_w, pool_scale, seq_len):
    m = h.shape[0]
    tm = ROW_TM
    return pl.pallas_call(
        functools.partial(_pool_body, tiles_per_seq=seq_len // tm),
        grid=(m // tm,),
        in_specs=[pl.BlockSpec((tm, POOL_WIDTH), lambda i: (jnp.maximum(i - 1, 0), 0)),
                  pl.BlockSpec((tm, POOL_WIDTH), lambda i: (i, 0)),
                  pl.BlockSpec((len(POOL_WINDOWS), POOL_GROUP, POOL_GROUP), lambda i: (0, 0, 0)),
                  pl.BlockSpec((1, POOL_WIDTH), lambda i: (0, 0))],
        out_specs=pl.BlockSpec((tm, POOL_WIDTH), lambda i: (i, 0)),
        out_shape=jax.ShapeDtypeStruct((m, POOL_WIDTH), bf16),
        compiler_params=_cparams(("parallel",)),
        name="pool_mixer",
    )(h, h, pool_w, pool_scale.reshape(1, -1))


def _sortable(x):
    bits = pltpu.bitcast(x, i32)
    return bits ^ ((bits >> 31) & 0x7FFFFFFF)


def _dsa_index_body(qi_ref, ki_ref, wi_ref, o_ref, key_ref, hi_ref, lo_ref, *, topk):
    tq, tk, ks = IDX_TQ, IDX_TK, IDX_KS
    n_chunks = key_ref.shape[0]
    q0 = pl.program_id(1) * tq
    n_live = (q0 + tq + tk - 1) // tk
    wt = wi_ref[...].T
    qpos = q0 + lax.broadcasted_iota(i32, (1, tq), 1)
    krow = lax.broadcasted_iota(i32, (ks, 1), 0)

    def score_chunk(c, carry):
        k0 = pl.multiple_of(c * tk, tk)
        for s in range(tk // ks):
            kic = ki_ref[pl.ds(k0 + s * ks, ks), :]
            acc = jnp.zeros((ks, tq), f32)
            for h in range(IDX_HEADS):
                qh = qi_ref[:, h * IDX_HEAD_DIM:(h + 1) * IDX_HEAD_DIM]
                sc = lax.dot_general(kic, qh, (((1,), (1,)), ((), ())), preferred_element_type=f32)
                acc = acc + wt[h:h + 1, :] * jnp.maximum(sc, 0.0)
            acc = jnp.where(krow + (k0 + s * ks) <= qpos, acc, -jnp.inf)
            key_ref[c, s * ks:(s + 1) * ks, :] = _sortable(acc)
        return carry

    lax.fori_loop(0, n_live, score_chunk, 0)

    psub = 16
    n_part = 4
    half = 1 << 15

    def count_ge(ref, test):
        test_b = jnp.broadcast_to(test.astype(jnp.int16), (psub, tq))

        def count_chunk(c, parts):
            parts = list(parts)
            for r in range(tk // psub):
                kk = ref[c, r * psub:(r + 1) * psub, :]
                parts[r % n_part] = parts[r % n_part] + jnp.where(kk >= test_b, jnp.int16(1), jnp.int16(0))
            return tuple(parts)

        parts = lax.fori_loop(0, n_live, count_chunk, tuple(jnp.zeros((psub, tq), jnp.int16) for _ in range(n_part)))
        tot = parts[0].astype(i32) + parts[1].astype(i32) + parts[2].astype(i32) + parts[3].astype(i32)
        return jnp.sum(tot, axis=0, keepdims=True)

    def radix_max(ref, need):
        def bit_step(b, cand):
            test = cand | jnp.left_shift(jnp.int32(1), 15 - b)
            cnt = count_ge(ref, test - half)
            return jnp.where(cnt >= need, test, cand)
        return lax.fori_loop(0, 16, bit_step, jnp.zeros((1, tq), i32)) - half

    def split_chunk(c, carry):
        kk = key_ref[c]
        hi_ref[c] = (kk >> 16).astype(jnp.int16)
        lo_ref[c] = ((kk & 0xFFFF) - half).astype(jnp.int16)
        return carry

    lax.fori_loop(0, n_live, split_chunk, 0)
    top = radix_max(hi_ref, topk)
    above = jnp.where(top == half - 1, 0, count_ge(hi_ref, jnp.minimum(top + 1, half - 1)))
    top_b = jnp.broadcast_to(top.astype(jnp.int16), (tk, tq))

    def bucket_chunk(c, carry):
        hi_ref[c] = jnp.where(hi_ref[c] == top_b, lo_ref[c], jnp.int16(-half))
        return carry

    lax.fori_loop(0, n_live, bucket_chunk, 0)
    low = radix_max(hi_ref, topk - above)
    neg_inf_key = INT_MIN + 0x007FFFFF
    thr = jnp.left_shift(top, 16) + (low + half)
    thr = jnp.where(qpos + 1 <= topk, neg_inf_key + 1, jnp.maximum(thr, neg_inf_key + 1))

    def bias_chunk(c, carry):
        o_ref[0, c] = jnp.where(key_ref[c] >= thr, 0.0, NEG_BIG).astype(o_ref.dtype)
        return carry

    lax.fori_loop(0, n_live, bias_chunk, 0)

    def dead_chunk(c, carry):
        o_ref[0, c] = jnp.full((tk, tq), NEG_BIG, o_ref.dtype)
        return carry

    lax.fori_loop(n_live, n_chunks, dead_chunk, 0)


def dsa_index(hq, qi_col, ki, wi, batch, seq_len):
    tq, tk = IDX_TQ, IDX_TK
    nq, nc = seq_len // tq, seq_len // tk
    topk = min(TOPK_MAX, seq_len // 4)
    return pl.pallas_call(
        functools.partial(_dsa_index_body, topk=topk),
        grid=(batch, nq),
        in_specs=[pl.BlockSpec((tq, IDX_HEADS * IDX_HEAD_DIM), lambda b, i: (b * nq + i, qi_col)),
                  pl.BlockSpec((seq_len, IDX_HEAD_DIM), lambda b, i: (b, 0)),
                  pl.BlockSpec((tq, LANES), lambda b, i: (b * nq + i, 0))],
        out_specs=pl.BlockSpec((1, nc, tk, tq), lambda b, i: (b * nq + i, 0, 0, 0)),
        out_shape=jax.ShapeDtypeStruct((batch * nq, nc, tk, tq), bf16),
        scratch_shapes=[pltpu.VMEM((nc, tk, tq), i32),
                        pltpu.VMEM((nc, tk, tq), jnp.int16),
                        pltpu.VMEM((nc, tk, tq), jnp.int16)],
        compiler_params=_cparams(("parallel", "parallel")),
        name="dsa_index",
    )(hq, ki, wi)


def _dsa_attn_body(qt_ref, ct_ref, q_ref, k_ref, vt_ref, bias_ref, o_ref, m_ref, acc_ref):
    tq, tk, dh = ATT_TQ, IDX_TK, ATTN_HEAD_DIM
    s = pl.program_id(1)
    i, c = qt_ref[s], ct_ref[s]
    last = ((i + 1) * tq - 1) // tk

    @pl.when(c == 0)
    def _():
        m_ref[...] = jnp.full(m_ref.shape, NEG_BIG, f32)
        acc_ref[...] = jnp.zeros(acc_ref.shape, f32)

    bias_t = jnp.concatenate([bias_ref[j, 0] for j in range(bias_ref.shape[0])], axis=1).astype(f32)

    def logits(h):
        hs = slice(h * dh, (h + 1) * dh)
        return lax.dot_general(k_ref[:, hs], q_ref[:, hs], (((1,), (1,)), ((), ())),
                               preferred_element_type=f32) + bias_t

    st_next = logits(0)
    for h in range(ATTN_HEADS):
        st = st_next
        if h + 1 < ATTN_HEADS:
            st_next = logits(h + 1)
        m_old = m_ref[h]
        m_new = jnp.maximum(m_old, jnp.max(st, axis=0, keepdims=True))
        p = jnp.exp2((st - m_new).astype(bf16))
        a = jnp.exp2(m_old - m_new)
        acc_ref[h] = a * acc_ref[h] + jnp.dot(vt_ref[h], p, preferred_element_type=f32)
        m_ref[h] = m_new

    @pl.when(c == last)
    def _():
        for h in range(ATTN_HEADS):
            acc = acc_ref[h]
            o_ref[:, h * dh:(h + 1) * dh] = (acc[:dh] / acc[dh:dh + 1]).T.astype(o_ref.dtype)


def dsa_attention(hq, q_col, k_col, vt, bias, batch, seq_len):
    tq, tk = ATT_TQ, IDX_TK
    nq, nc = seq_len // tq, seq_len // tk
    sub = tq // IDX_TQ
    steps = [(i, c) for i in range(nq) for c in range(((i + 1) * tq - 1) // tk + 1)]
    q_tab = jnp.asarray(np.array([s[0] for s in steps], np.int32))
    c_tab = jnp.asarray(np.array([s[1] for s in steps], np.int32))
    grid_spec = pltpu.PrefetchScalarGridSpec(
        num_scalar_prefetch=2,
        grid=(batch, len(steps)),
        in_specs=[pl.BlockSpec((tq, ATTN_WIDTH), lambda b, s, qt, ct: (b * nq + qt[s], q_col)),
                  pl.BlockSpec((tk, ATTN_WIDTH), lambda b, s, qt, ct: (b * nc + ct[s], k_col)),
                  pl.BlockSpec((ATTN_HEADS, VT_ROWS, tk), lambda b, s, qt, ct: (0, 0, b * nc + ct[s])),
                  pl.BlockSpec((sub, 1, tk, IDX_TQ), lambda b, s, qt, ct: (b * nq + qt[s], ct[s], 0, 0))],
        out_specs=pl.BlockSpec((tq, ATTN_WIDTH), lambda b, s, qt, ct: (b * nq + qt[s], 0)),
        scratch_shapes=[pltpu.VMEM((ATTN_HEADS, 1, tq), f32),
                        pltpu.VMEM((ATTN_HEADS, VT_ROWS, tq), f32)],
    )
    return pl.pallas_call(
        _dsa_attn_body,
        grid_spec=grid_spec,
        out_shape=jax.ShapeDtypeStruct((batch * seq_len, ATTN_WIDTH), bf16),
        compiler_params=_cparams(("parallel", "arbitrary")),
        name="dsa_attention",
    )(q_tab, c_tab, hq, hq, vt, bias)


def _out_proj_ln_body(a_ref, o_ref, w_ref, x_ref, g_ref, b_ref, y_ref):
    acc = jnp.dot(a_ref[...], w_ref[:POOL_WIDTH, :], preferred_element_type=f32)
    acc = acc + jnp.dot(o_ref[...], w_ref[POOL_WIDTH:, :], preferred_element_type=f32)
    y_ref[...] = _layer_norm(ALPHA * x_ref[...] + acc, g_ref[...], b_ref[...])


def out_proj_ln(a, o, w_out_b, x, g, b):
    m = x.shape[0]
    tm = ROW_TM
    return pl.pallas_call(
        _out_proj_ln_body,
        grid=(m // tm,),
        in_specs=[pl.BlockSpec((tm, POOL_WIDTH), lambda i: (i, 0)),
                  pl.BlockSpec((tm, ATTN_WIDTH), lambda i: (i, 0)),
                  pl.BlockSpec((POOL_WIDTH + ATTN_WIDTH, D_MODEL), lambda i: (0, 0)),
                  pl.BlockSpec((tm, D_MODEL), lambda i: (i, 0)),
                  pl.BlockSpec((1, D_MODEL), lambda i: (0, 0)),
                  pl.BlockSpec((1, D_MODEL), lambda i: (0, 0))],
        out_specs=pl.BlockSpec((tm, D_MODEL), lambda i: (i, 0)),
        out_shape=jax.ShapeDtypeStruct((m, D_MODEL), f32),
        compiler_params=_cparams(("parallel",)),
        name="out_proj_ln",
    )(a, o, w_out_b, x, g.reshape(1, -1), b.reshape(1, -1))


def _conv_out_ln_body(gb_ref, gc_ref, hv_ref, gch_ref, hvh_ref, cw_ref, w_ref, x_ref, g_ref, b_ref, y_ref,
                      *, tiles_per_seq):
    tm = gb_ref.shape[0]
    first = (pl.program_id(0) % tiles_per_seq) == 0
    u = gc_ref[...].astype(f32) * hv_ref[...].astype(f32)
    uh = gch_ref[...].astype(f32) * hvh_ref[...].astype(f32)
    uh = jnp.where(first, 0.0, uh)
    hr = uh.shape[0]
    row = lax.broadcasted_iota(i32, (tm, 1), 0)
    u1 = jnp.where(row == 0, uh[hr - 1:hr, :], pltpu.roll(u, 1, 0))
    u2 = jnp.where(row == 0, uh[hr - 2:hr - 1, :],
                   jnp.where(row == 1, uh[hr - 1:hr, :], pltpu.roll(u, 2, 0)))
    z = cw_ref[0:1, :] * u2 + cw_ref[1:2, :] * u1 + cw_ref[2:3, :] * u
    gz = (gb_ref[...].astype(f32) * z).astype(bf16)
    acc = jnp.dot(gz, w_ref[...], preferred_element_type=f32)
    y_ref[...] = _layer_norm(ALPHA * x_ref[...] + acc, g_ref[...], b_ref[...])


def conv_out_ln(h1, conv_w, w_out_b, x, g, b, seq_len):
    m = x.shape[0]
    tm = ROW_TM
    halo = 16
    hb = tm // halo

    def halo_idx(col):
        return lambda i: (jnp.maximum(i * hb - 1, 0), col)

    return pl.pallas_call(
        functools.partial(_conv_out_ln_body, tiles_per_seq=seq_len // tm),
        grid=(m // tm,),
        in_specs=[pl.BlockSpec((tm, D_MODEL), lambda i: (i, 0)),
                  pl.BlockSpec((tm, D_MODEL), lambda i: (i, 1)),
                  pl.BlockSpec((tm, D_MODEL), lambda i: (i, 2)),
                  pl.BlockSpec((halo, D_MODEL), halo_idx(1)),
                  pl.BlockSpec((halo, D_MODEL), halo_idx(2)),
                  pl.BlockSpec((CONV_WIDTH, D_MODEL), lambda i: (0, 0)),
                  pl.BlockSpec((D_MODEL, D_MODEL), lambda i: (0, 0)),
                  pl.BlockSpec((tm, D_MODEL), lambda i: (i, 0)),
                  pl.BlockSpec((1, D_MODEL), lambda i: (0, 0)),
                  pl.BlockSpec((1, D_MODEL), lambda i: (0, 0))],
        out_specs=pl.BlockSpec((tm, D_MODEL), lambda i: (i, 0)),
        out_shape=jax.ShapeDtypeStruct((m, D_MODEL), f32),
        compiler_params=_cparams(("parallel",)),
        name="conv_out_ln",
    )(h1, h1, h1, h1, h1, conv_w, w_out_b, x, g.reshape(1, -1), b.reshape(1, -1))


ROUTE_E0, ROUTE_E1, ROUTE_C0, ROUTE_C1, ROUTE_R0, ROUTE_R1 = range(6)


def _split_bf16(x):
    hi = x.astype(bf16)
    lo = (x - hi.astype(f32)).astype(bf16)
    return hi, lo


def _router_body(x_ref, w_ref, b_ref, o_ref, cnt_ref, xs_ref):
    i = pl.program_id(0)
    for j in range(SLAB):
        xs_ref[pl.ds(j, x_ref.shape[0], stride=SLAB), :] = x_ref[:, j * LANES:(j + 1) * LANES]

    @pl.when(i == 0)
    def _():
        cnt_ref[...] = jnp.zeros(cnt_ref.shape, f32)

    xh, xl = _split_bf16(x_ref[...])
    wh, wl = _split_bf16(w_ref[...])
    lg = (jnp.dot(xh, wh, preferred_element_type=f32) + jnp.dot(xh, wl, preferred_element_type=f32)
          + jnp.dot(xl, wh, preferred_element_type=f32)) + b_ref[...]
    tm, nl = lg.shape
    lane = lax.broadcasted_iota(i32, (tm, nl), 1)
    neg = -jnp.inf
    is_g = lane < N_GROUPS
    gl = jnp.where(is_g, lg, neg)
    gmax = jnp.max(gl, axis=1, keepdims=True)
    gidx = jnp.min(jnp.where(gl == gmax, lane, nl), axis=1, keepdims=True)
    gw = 1.0 / jnp.sum(jnp.where(is_g, jnp.exp(lg - gmax), 0.0), axis=1, keepdims=True)
    e_lo = N_GROUPS + gidx * EXPERTS_PER_GROUP
    in_grp = (lane >= e_lo) & (lane < e_lo + EXPERTS_PER_GROUP)
    el = jnp.where(in_grp, lg, neg)
    l0 = jnp.max(el, axis=1, keepdims=True)
    i0 = jnp.min(jnp.where(el == l0, lane, nl), axis=1, keepdims=True)
    el1 = jnp.where(lane == i0, neg, el)
    l1 = jnp.max(el1, axis=1, keepdims=True)
    i1 = jnp.min(jnp.where(el1 == l1, lane, nl), axis=1, keepdims=True)
    p1 = jnp.exp(l1 - l0)
    c0 = gw / (1.0 + p1)
    c1 = gw * p1 / (1.0 + p1)
    hot0, hot1 = lane == i0, lane == i1
    onehot = jnp.where(hot0, 1.0, jnp.where(hot1, 1.0, 0.0))
    before = lax.broadcasted_iota(i32, (tm, tm), 0) > lax.broadcasted_iota(i32, (tm, tm), 1)
    seen = jnp.dot(jnp.where(before, 1.0, 0.0).astype(bf16), onehot.astype(bf16),
                   preferred_element_type=f32) + cnt_ref[...]
    r0 = jnp.sum(jnp.where(hot0, seen, 0.0), axis=1, keepdims=True)
    r1 = jnp.sum(jnp.where(hot1, seen, 0.0), axis=1, keepdims=True)
    cnt_ref[...] += jnp.sum(onehot, axis=0, keepdims=True)
    cols = {ROUTE_E0: (i0 - N_GROUPS).astype(f32), ROUTE_E1: (i1 - N_GROUPS).astype(f32),
            ROUTE_C0: c0, ROUTE_C1: c1, ROUTE_R0: r0, ROUTE_R1: r1}
    out = jnp.zeros((tm, nl), f32)
    for k, v in cols.items():
        out = jnp.where(lane == k, v, out)
    o_ref[...] = out


def moe_router(x, w_group, b_group, w_expert, b_expert):
    m = x.shape[0]
    tm = ROW_TM
    nl = LANES
    w = jnp.zeros((D_MODEL, nl), f32).at[:, :N_GROUPS].set(w_group).at[:, N_GROUPS:N_GROUPS + N_EXPERTS].set(w_expert)
    bias = jnp.zeros((1, nl), f32).at[0, :N_GROUPS].set(b_group).at[0, N_GROUPS:N_GROUPS + N_EXPERTS].set(b_expert)
    return pl.pallas_call(
        _router_body,
        grid=(m // tm,),
        in_specs=[pl.BlockSpec((tm, D_MODEL), lambda i: (i, 0)),
                  pl.BlockSpec((D_MODEL, nl), lambda i: (0, 0)),
                  pl.BlockSpec((1, nl), lambda i: (0, 0))],
        out_specs=[pl.BlockSpec((tm, nl), lambda i: (i, 0)),
                   pl.BlockSpec((1, nl), lambda i: (0, 0)),
                   pl.BlockSpec((tm * SLAB, LANES), lambda i: (i, 0))],
        out_shape=[jax.ShapeDtypeStruct((m, nl), f32),
                   jax.ShapeDtypeStruct((1, nl), f32),
                   jax.ShapeDtypeStruct((m * SLAB, LANES), f32)],
        compiler_params=_cparams(("arbitrary",)),
        name="moe_router",
    )(x, w, bias)


def _route_plan(route, counts_row, tm, n_tiles):
    counts = counts_row[0, N_GROUPS:N_GROUPS + N_EXPERTS].astype(i32)
    tiles_per = (counts + tm - 1) // tm
    tile_end = jnp.cumsum(tiles_per)
    tile_start = tile_end - tiles_per
    ri = route[:, :8].astype(i32)
    experts = jnp.arange(N_EXPERTS, dtype=i32)[None, :]

    def slot(e, r):
        return jnp.sum(jnp.where(e[:, None] == experts, tile_start[None, :], 0), axis=1) * tm + r

    pos0 = slot(ri[:, ROUTE_E0], ri[:, ROUTE_R0])
    pos1 = slot(ri[:, ROUTE_E1], ri[:, ROUTE_R1])
    tile_ids = jnp.arange(n_tiles, dtype=i32)
    n_valid = tile_end[-1]
    valid = tile_ids < n_valid
    tile_e = jnp.sum((tile_ids[:, None] >= tile_end[None, :]).astype(i32), axis=1)
    last_e = jnp.sum((n_valid - 1 >= tile_end).astype(i32))
    tile_e = jnp.where(valid, tile_e, last_e).astype(i32)
    prev_e = jnp.concatenate([jnp.full((1,), -1, i32), tile_e[:-1]])
    first = ((tile_e != prev_e) & valid).astype(i32)
    pad_lo = tile_start * tm + counts
    pad_hi = tile_end * tm
    return pos0, pos1, pad_lo, pad_hi, tile_e, first, valid.astype(i32)


def _start_row_gathers(src_hbm, dst_ref, sem, rows, row_of):
    def body(j, carry):
        for k in range(DMA_UNROLL):
            r = j * DMA_UNROLL + k
            pltpu.make_async_copy(src_hbm.at[pl.ds(row_of(r), 1), :], dst_ref.at[pl.ds(r, 1), :], sem).start()
        return carry
    lax.fori_loop(0, rows // DMA_UNROLL, body, 0)


def _wait_row_gathers(src_hbm, dst_ref, sem):
    rows = dst_ref.shape[0]
    pltpu.make_async_copy(src_hbm.at[pl.ds(0, rows), :], dst_ref, sem).wait()


def _slab_copy(xs_hbm, dst_ref, sem, token, row):
    return pltpu.make_async_copy(xs_hbm.at[pl.ds(pl.multiple_of(token * SLAB, SLAB), SLAB), :],
                                 dst_ref.at[pl.ds(row * SLAB, SLAB), :], sem)


def _moe_ffn_body(p0_ref, p1_ref, lo_ref, hi_ref, te_ref, first_ref, valid_ref,
                  x_hbm, wg_ref, wu_ref, wd_ref, y_ref,
                  src_ref, xg_ref, sem_ref, wgb_ref, wub_ref, wdb_ref):
    tm = MOE_TM
    t = pl.program_id(0)
    slot = t % 2
    n_tok = p0_ref.shape[0]
    is_valid = valid_ref[t] == 1

    @pl.when(t == 0)
    def _():
        def fill(n, carry):
            src_ref[p0_ref[n]] = n
            src_ref[p1_ref[n]] = n
            return carry
        lax.fori_loop(0, n_tok, fill, 0, unroll=8)

        def pad(p, carry):
            src_ref[p] = 0
            return carry
        for e in range(N_EXPERTS):
            lax.fori_loop(lo_ref[e], hi_ref[e], pad, 0)
        end = hi_ref[N_EXPERTS - 1]
        lax.fori_loop(end, end + tm, pad, 0)
        def first_tile(j, carry):
            for k in range(DMA_UNROLL):
                r = j * DMA_UNROLL + k
                _slab_copy(x_hbm, xg_ref.at[0], sem_ref.at[0], src_ref[r], r).start()
            return carry
        lax.fori_loop(0, tm // DMA_UNROLL, first_tile, 0)

    @pl.when(first_ref[t] == 1)
    def _():
        wgb_ref[...] = wg_ref[0, 0].astype(bf16)
        wub_ref[...] = wu_ref[0, 0].astype(bf16)
        wdb_ref[...] = wd_ref[0, 0].astype(bf16)

    @pl.when(is_valid)
    def _():
        _wait_row_gathers(x_hbm, xg_ref.at[slot], sem_ref.at[slot])
        nxt = (t + 1) * tm

        def prefetch(lo, hi):
            for r in range(lo, hi):
                _slab_copy(x_hbm, xg_ref.at[1 - slot], sem_ref.at[1 - slot], src_ref[nxt + r], r).start()

        xb = jnp.concatenate([xg_ref[slot, pl.ds(j, tm, stride=SLAB), :].astype(bf16) for j in range(SLAB)], axis=1)
        gate = jnp.dot(xb, wgb_ref[...], preferred_element_type=f32)
        prefetch(0, tm // 2)
        up = jnp.dot(xb, wub_ref[...], preferred_element_type=f32)
        prefetch(tm // 2, tm)
        hdn = (gate * jax.nn.sigmoid(gate) * up).astype(bf16)
        y_ref[...] = jnp.dot(hdn, wdb_ref[...], preferred_element_type=f32)

    @pl.when(jnp.logical_not(is_valid))
    def _():
        @pl.when(valid_ref[jnp.maximum(t - 1, 0)] == 1)
        def _():
            _wait_row_gathers(x_hbm, xg_ref.at[slot], sem_ref.at[slot])
        y_ref[...] = jnp.zeros(y_ref.shape, y_ref.dtype)


def moe_ffn(x, plan, w_gate, w_up, w_down, layer, n_tiles):
    tm = MOE_TM
    n_pre = len(plan)

    def w_idx(t, *pre):
        return (layer, pre[4][t], 0, 0)

    grid_spec = pltpu.PrefetchScalarGridSpec(
        num_scalar_prefetch=n_pre,
        grid=(n_tiles,),
        in_specs=[pl.BlockSpec(memory_space=pl.ANY),
                  pl.BlockSpec((1, 1, D_MODEL, D_FF_EXPERT), w_idx),
                  pl.BlockSpec((1, 1, D_MODEL, D_FF_EXPERT), w_idx),
                  pl.BlockSpec((1, 1, D_FF_EXPERT, D_MODEL), w_idx)],
        out_specs=pl.BlockSpec((tm, D_MODEL), lambda t, *pre: (t, 0)),
        scratch_shapes=[pltpu.SMEM((n_tiles * tm,), i32),
                        pltpu.VMEM((2, tm * SLAB, LANES), f32),
                        pltpu.SemaphoreType.DMA((2,)),
                        pltpu.VMEM((D_MODEL, D_FF_EXPERT), bf16),
                        pltpu.VMEM((D_MODEL, D_FF_EXPERT), bf16),
                        pltpu.VMEM((D_FF_EXPERT, D_MODEL), bf16)],
    )
    return pl.pallas_call(
        _moe_ffn_body,
        grid_spec=grid_spec,
        out_shape=jax.ShapeDtypeStruct((n_tiles * tm, D_MODEL), f32),
        compiler_params=_cparams(("arbitrary",)),
        name="moe_ffn",
    )(*plan, x, w_gate, w_up, w_down)


def _combine_ln_body(p0_ref, p1_ref, y_hbm, x_ref, r_ref, g_ref, b_ref, *rest, with_bf16):
    o_ref = rest[0]
    yg_ref, sem_ref = rest[-2:]
    tm = ROW_TM
    t = pl.program_id(0)
    nt = pl.num_programs(0)
    slot = t % 2
    p_refs = (p0_ref, p1_ref)

    @pl.when(t == 0)
    def _():
        for j in range(2):
            _start_row_gathers(y_hbm, yg_ref.at[0, j], sem_ref.at[0, j], tm, lambda r: p_refs[j][r])

    for j in range(2):
        _wait_row_gathers(y_hbm, yg_ref.at[slot, j], sem_ref.at[slot, j])

    nxt = jnp.minimum(t + 1, nt - 1) * tm

    def prefetch(j, lo, hi):
        for r in range(lo, hi):
            pltpu.make_async_copy(y_hbm.at[pl.ds(p_refs[j][nxt + r], 1), :],
                                  yg_ref.at[1 - slot, j, pl.ds(r, 1), :], sem_ref.at[1 - slot, j]).start()

    r = r_ref[...]
    prefetch(0, 0, tm // 2)
    f = r[:, ROUTE_C0:ROUTE_C0 + 1] * yg_ref[slot, 0] + r[:, ROUTE_C1:ROUTE_C1 + 1] * yg_ref[slot, 1]
    prefetch(0, tm // 2, tm)
    z = ALPHA * x_ref[...] + f
    prefetch(1, 0, tm // 2)
    y = _layer_norm(z, g_ref[...], b_ref[...])
    prefetch(1, tm // 2, tm)
    o_ref[...] = y
    if with_bf16:
        rest[1][...] = y.astype(bf16)

    @pl.when(t == nt - 1)
    def _():
        for j in range(2):
            _wait_row_gathers(y_hbm, yg_ref.at[1 - slot, j], sem_ref.at[1 - slot, j])


def combine_ln(y_sorted, pos0, pos1, route, x, g, b, with_bf16):
    m = x.shape[0]
    tm = ROW_TM
    row = lambda t, p0, p1: (t, 0)
    fixed = lambda t, p0, p1: (0, 0)
    out_specs = [pl.BlockSpec((tm, D_MODEL), row)]
    out_shape = [jax.ShapeDtypeStruct((m, D_MODEL), f32)]
    if with_bf16:
        out_specs.append(pl.BlockSpec((tm, D_MODEL), row))
        out_shape.append(jax.ShapeDtypeStruct((m, D_MODEL), bf16))
    grid_spec = pltpu.PrefetchScalarGridSpec(
        num_scalar_prefetch=2,
        grid=(m // tm,),
        in_specs=[pl.BlockSpec(memory_space=pl.ANY),
                  pl.BlockSpec((tm, D_MODEL), row),
                  pl.BlockSpec((tm, LANES), row),
                  pl.BlockSpec((1, D_MODEL), fixed),
                  pl.BlockSpec((1, D_MODEL), fixed)],
        out_specs=out_specs,
        scratch_shapes=[pltpu.VMEM((2, 2, tm, D_MODEL), f32),
                        pltpu.SemaphoreType.DMA((2, 2))],
    )
    return pl.pallas_call(
        functools.partial(_combine_ln_body, with_bf16=with_bf16),
        grid_spec=grid_spec,
        out_shape=out_shape,
        compiler_params=_cparams(("arbitrary",)),
        name="moe_combine_ln",
    )(pos0, pos1, y_sorted, x, route, g.reshape(1, -1), b.reshape(1, -1))


def hier_moe_ln(x, w_group, b_group, w_expert, b_expert, w_gate, w_up, w_down, layer, g, b, with_bf16):
    n = x.shape[0]
    n_tiles = 2 * n // MOE_TM + N_EXPERTS
    route, counts, xs = moe_router(x, w_group, b_group, w_expert, b_expert)
    pos0, pos1, pad_lo, pad_hi, tile_e, first, valid = _route_plan(route, counts, MOE_TM, n_tiles)
    y_sorted = moe_ffn(xs, (pos0, pos1, pad_lo, pad_hi, tile_e, first, valid), w_gate, w_up, w_down, layer, n_tiles)
    return combine_ln(y_sorted, pos0, pos1, route, x, g, b, with_bf16)


def kernel(x, ab_w_in, ab_idx_k_ln_g, ab_idx_k_ln_b, ab_pool_w, ab_pool_scale, ab_w_out, c_w_in, c_conv_w, c_w_out,
           ln_mix_g, ln_mix_b, ln_ffn_g, ln_ffn_b, moe_w_group, moe_b_group, moe_w_expert, moe_b_expert,
           moe_w_gate, moe_w_up, moe_w_down):
    batch, seq_len, d = x.shape
    n = batch * seq_len
    xf = x.reshape(n, d)
    moe = lambda layer: (moe_w_group[layer], moe_b_group[layer], moe_w_expert[layer], moe_b_expert[layer],
                         moe_w_gate, moe_w_up, moe_w_down, layer)

    xb = cast_bf16(xf)
    w_in = ab_w_in[0]
    w_in_b = cast_bf16(w_in)
    hq = matmul_bf16(xb, w_in_b, (0, 1, 2, 4), scales=(1.0, ATTN_HEAD_DIM ** -0.5 * LOG2E, 1.0, 1.0))
    vt = value_t(cast_bf16(w_in[:, V_COL * COL:(V_COL + 1) * COL].T), xb)
    tail = 5 * COL
    w_wi = jnp.pad(w_in[:, tail + IDX_HEAD_DIM:], ((0, 0), (0, LANES - IDX_HEADS)))
    ki, wi = idx_proj(xb, w_in[:, tail:tail + IDX_HEAD_DIM], w_wi, ab_idx_k_ln_g[0], ab_idx_k_ln_b[0])
    a = pool_mixer(hq, ab_pool_w[0], ab_pool_scale[0], seq_len)
    bias = dsa_index(hq, 3, ki, wi, batch, seq_len)
    o = dsa_attention(hq, 1, 2, vt, bias, batch, seq_len)
    x1 = out_proj_ln(a, o, cast_bf16(ab_w_out[0]), xf, ln_mix_g[0], ln_mix_b[0])
    x2, x2b = hier_moe_ln(x1, *moe(0), ln_ffn_g[0], ln_ffn_b[0], True)

    h1 = matmul_bf16(x2b, cast_bf16(c_w_in[0]), tuple(range(3 * D_MODEL // MM_TN)))
    x3 = conv_out_ln(h1, c_conv_w[0], cast_bf16(c_w_out[0]), x2, ln_mix_g[1], ln_mix_b[1], seq_len)
    (x4,) = hier_moe_ln(x3, *moe(1), ln_ffn_g[1], ln_ffn_b[1], False)
    return x4.reshape(batch, seq_len, d)
```

```python
import functools
import math

import numpy as np
import jax
import jax.numpy as jnp
from jax import lax
from jax.experimental import pallas as pl
from jax.experimental.pallas import tpu as pltpu

f32 = jnp.float32
bf16 = jnp.bfloat16
i32 = jnp.int32

D_MODEL = 2048
POOL_WINDOWS = (2, 4, 8, 16)
POOL_WIDTH = 1024
POOL_GROUP = 256
ATTN_HEADS = 8
ATTN_HEAD_DIM = 128
ATTN_WIDTH = 1024
IDX_HEADS = 16
IDX_HEAD_DIM = 64
TOPK_MAX = 256
CONV_WIDTH = 3
N_GROUPS = 4
EXPERTS_PER_GROUP = 8
N_EXPERTS = 32
D_FF_EXPERT = 512
DEPTH = 2
ALPHA = (2 * DEPTH) ** 0.25
LN_EPS = 1e-5
COL = 1024
V_COL = 3

LANES = 128
VMEM_LIMIT = 56 * 1024 * 1024
MM_TM, MM_TN = 512, 1024
ROW_TM = 256
IDX_TQ, IDX_TK = 256, 512
IDX_KS = 128
ATT_TQ = 256
ATT_RB = 128
VT_ROWS = ATTN_HEAD_DIM + 16
MOE_TM = 256
DMA_UNROLL = 8
NEG_BIG = -1e30
INT_MIN = -(2 ** 31)
LOG2E = math.log2(math.e)


def _cparams(sem):
    return pltpu.CompilerParams(dimension_semantics=sem, vmem_limit_bytes=VMEM_LIMIT)


def _layer_norm(z, g, b):
    mu = jnp.mean(z, axis=-1, keepdims=True)
    zc = z - mu
    var = jnp.mean(zc * zc, axis=-1, keepdims=True)
    return zc * lax.rsqrt(var + LN_EPS) * g + b


def _cast_body(x_ref, o_ref):
    o_ref[...] = x_ref[...].astype(o_ref.dtype)


def cast_bf16(x, tm=512):
    m, n = x.shape
    return pl.pallas_call(
        _cast_body,
        grid=(m // tm,),
        in_specs=[pl.BlockSpec((tm, n), lambda i: (i, 0))],
        out_specs=pl.BlockSpec((tm, n), lambda i: (i, 0)),
        out_shape=jax.ShapeDtypeStruct((m, n), bf16),
        compiler_params=_cparams(("parallel",)),
        name="cast_bf16",
    )(x)


def _mm_body(x_ref, w_ref, o_ref, *, scales):
    acc = jnp.dot(x_ref[...], w_ref[...], preferred_element_type=f32)
    j = pl.program_id(0)
    scale = jnp.float32(1.0)
    for jj, s in enumerate(scales):
        if s != 1.0:
            scale = jnp.where(j == jj, jnp.float32(s), scale)
    o_ref[...] = (acc * scale).astype(o_ref.dtype)


def matmul_bf16(x, w, col_blocks, scales=None):
    m, k = x.shape
    nb = len(col_blocks)
    scales = tuple(scales) if scales is not None else (1.0,) * nb
    skip = [c for c in range(col_blocks[0], col_blocks[-1] + 1) if c not in col_blocks]
    assert len(skip) <= 1 and list(col_blocks) == sorted(col_blocks)
    first = col_blocks[0]

    def w_col(j):
        c = j + first
        return c + (c >= skip[0]).astype(i32) if skip else c

    return pl.pallas_call(
        functools.partial(_mm_body, scales=scales),
        grid=(nb, m // MM_TM),
        in_specs=[pl.BlockSpec((MM_TM, k), lambda j, i: (i, 0)),
                  pl.BlockSpec((k, MM_TN), lambda j, i: (0, w_col(j)))],
        out_specs=pl.BlockSpec((MM_TM, MM_TN), lambda j, i: (i, j)),
        out_shape=jax.ShapeDtypeStruct((m, nb * MM_TN), bf16),
        compiler_params=_cparams(("parallel", "parallel")),
        name="proj_matmul",
    )(x, w)


def _value_t_body(wt_ref, x_ref, o_ref):
    vt = lax.dot_general(wt_ref[...], x_ref[...], (((1,), (1,)), ((), ())), preferred_element_type=f32)
    dh = ATTN_HEAD_DIM
    for h in range(ATTN_HEADS):
        o_ref[h, :dh, :] = vt[h * dh:(h + 1) * dh, :].astype(o_ref.dtype)
        o_ref[h, dh:, :] = jnp.ones((VT_ROWS - dh, o_ref.shape[2]), o_ref.dtype)


def value_t(wt, x):
    c, k = wt.shape
    m = x.shape[0]
    return pl.pallas_call(
        _value_t_body,
        grid=(m // MM_TM,),
        in_specs=[pl.BlockSpec((c, k), lambda i: (0, 0)),
                  pl.BlockSpec((MM_TM, k), lambda i: (i, 0))],
        out_specs=pl.BlockSpec((ATTN_HEADS, VT_ROWS, MM_TM), lambda i: (0, 0, i)),
        out_shape=jax.ShapeDtypeStruct((ATTN_HEADS, VT_ROWS, m), bf16),
        compiler_params=_cparams(("parallel",)),
        name="value_t",
    )(wt, x)


def _idx_proj_body(x_ref, wk_ref, ww_ref, g_ref, b_ref, ki_ref, wi_ref, *, wi_scale):
    x = x_ref[...]
    hk = jnp.dot(x, wk_ref[...].astype(bf16), preferred_element_type=f32)
    hw = jnp.dot(x, ww_ref[...].astype(bf16), preferred_element_type=f32)
    ki_ref[...] = _layer_norm(hk, g_ref[...], b_ref[...]).astype(ki_ref.dtype)
    wi_ref[...] = hw * wi_scale


def idx_proj(xb, w_ki, w_wi, g, b):
    m, k = xb.shape
    tm = MM_TM
    wi_scale = (IDX_HEADS ** -0.5) * (IDX_HEAD_DIM ** -0.5)
    return pl.pallas_call(
        functools.partial(_idx_proj_body, wi_scale=wi_scale),
        grid=(m // tm,),
        in_specs=[pl.BlockSpec((tm, k), lambda i: (i, 0)),
                  pl.BlockSpec((k, IDX_HEAD_DIM), lambda i: (0, 0)),
                  pl.BlockSpec((k, LANES), lambda i: (0, 0)),
                  pl.BlockSpec((1, IDX_HEAD_DIM), lambda i: (0, 0)),
                  pl.BlockSpec((1, IDX_HEAD_DIM), lambda i: (0, 0))],
        out_specs=[pl.BlockSpec((tm, IDX_HEAD_DIM), lambda i: (i, 0)),
                   pl.BlockSpec((tm, LANES), lambda i: (i, 0))],
        out_shape=[jax.ShapeDtypeStruct((m, IDX_HEAD_DIM), bf16),
                   jax.ShapeDtypeStruct((m, LANES), f32)],
        compiler_params=_cparams(("parallel",)),
        name="idx_proj",
    )(xb, w_ki, w_wi, g.reshape(1, -1), b.reshape(1, -1))


def _pool_body(up_ref, uc_ref, pw_ref, ps_ref, o_ref, *, tiles_per_seq):
    tm = uc_ref.shape[0]
    i = pl.program_id(0)
    t0 = (i % tiles_per_seq) * tm
    row = lax.broadcasted_iota(i32, (tm, 2 * tm), 0)
    col = lax.broadcasted_iota(i32, (tm, 2 * tm), 1) - tm
    lo_ok = col + t0 >= 0
    pos = t0 + lax.broadcasted_iota(i32, (tm, 1), 0)
    for g, win in enumerate(POOL_WINDOWS):
        cs = slice(g * POOL_GROUP, (g + 1) * POOL_GROUP)
        band = jnp.where((col <= row) & (col > row - win) & lo_ok, 1.0, 0.0).astype(bf16)
        uc = uc_ref[:, cs]
        ucat = jnp.concatenate([up_ref[:, cs], uc], axis=0)
        ssum = jnp.dot(band, ucat, preferred_element_type=f32)
        cnt = jnp.minimum(pos + 1, win).astype(f32)
        pooled = ssum / cnt - uc.astype(f32)
        a = jnp.dot(pooled.astype(bf16), pw_ref[g].astype(bf16), preferred_element_type=f32)
        o_ref[:, cs] = (a * ps_ref[:, cs]).astype(o_ref.dtype)


def pool_mixer(h, pool_w, pool_scale, seq_len):
    m = h.shape[0]
    tm = ROW_TM
    return pl.pallas_call(
        functools.partial(_pool_body, tiles_per_seq=seq_len // tm),
        grid=(m // tm,),
        in_specs=[pl.BlockSpec((tm, POOL_WIDTH), lambda i: (jnp.maximum(i - 1, 0), 0)),
                  pl.BlockSpec((tm, POOL_WIDTH), lambda i: (i, 0)),
                  pl.BlockSpec((len(POOL_WINDOWS), POOL_GROUP, POOL_GROUP), lambda i: (0, 0, 0)),
                  pl.BlockSpec((1, POOL_WIDTH), lambda i: (0, 0))],
        out_specs=pl.BlockSpec((tm, POOL_WIDTH), lambda i: (i, 0)),
        out_shape=jax.ShapeDtypeStruct((m, POOL_WIDTH), bf16),
        compiler_params=_cparams(("parallel",)),
        name="pool_mixer",
    )(h, h, pool_w, pool_scale.reshape(1, -1))


def _sortable(x):
    bits = pltpu.bitcast(x, i32)
    return bits ^ ((bits >> 31) & 0x7FFFFFFF)


def _dsa_index_body(qi_ref, ki_ref, wi_ref, o_ref, key_ref, hi_ref, lo_ref, *, topk):
    tq, tk, ks = IDX_TQ, IDX_TK, IDX_KS
    n_chunks = key_ref.shape[0]
    q0 = pl.program_id(1) * tq
    n_live = (q0 + tq + tk - 1) // tk
    wt = wi_ref[...].T
    qpos = q0 + lax.broadcasted_iota(i32, (1, tq), 1)
    krow = lax.broadcasted_iota(i32, (ks, 1), 0)

    def score_chunk(c, carry):
        k0 = pl.multiple_of(c * tk, tk)
        for s in range(tk // ks):
            kic = ki_ref[pl.ds(k0 + s * ks, ks), :]
            acc = jnp.zeros((ks, tq), f32)
            for h in range(IDX_HEADS):
                qh = qi_ref[:, h * IDX_HEAD_DIM:(h + 1) * IDX_HEAD_DIM]
                sc = lax.dot_general(kic, qh, (((1,), (1,)), ((), ())), preferred_element_type=f32)
                acc = acc + wt[h:h + 1, :] * jnp.maximum(sc, 0.0)
            acc = jnp.where(krow + (k0 + s * ks) <= qpos, acc, -jnp.inf)
            key_ref[c, s * ks:(s + 1) * ks, :] = _sortable(acc)
        return carry

    lax.fori_loop(0, n_live, score_chunk, 0)

    psub = 16
    n_part = 4
    half = 1 << 15

    def count_ge(ref, test):
        test_b = jnp.broadcast_to(test.astype(jnp.int16), (psub, tq))

        def count_chunk(c, parts):
            parts = list(parts)
            for r in range(tk // psub):
                kk = ref[c, r * psub:(r + 1) * psub, :]
                parts[r % n_part] = parts[r % n_part] + jnp.where(kk >= test_b, jnp.int16(1), jnp.int16(0))
            return tuple(parts)

        parts = lax.fori_loop(0, n_live, count_chunk, tuple(jnp.zeros((psub, tq), jnp.int16) for _ in range(n_part)))
        tot = parts[0].astype(i32) + parts[1].astype(i32) + parts[2].astype(i32) + parts[3].astype(i32)
        return jnp.sum(tot, axis=0, keepdims=True)

    def radix_max(ref, need):
        def bit_step(b, cand):
            test = cand | jnp.left_shift(jnp.int32(1), 15 - b)
            cnt = count_ge(ref, test - half)
            return jnp.where(cnt >= need, test, cand)
        return lax.fori_loop(0, 16, bit_step, jnp.zeros((1, tq), i32)) - half

    def split_chunk(c, carry):
        kk = key_ref[c]
        hi_ref[c] = (kk >> 16).astype(jnp.int16)
        lo_ref[c] = ((kk & 0xFFFF) - half).astype(jnp.int16)
        return carry

    lax.fori_loop(0, n_live, split_chunk, 0)
    top = radix_max(hi_ref, topk)
    above = jnp.where(top == half - 1, 0, count_ge(hi_ref, jnp.minimum(top + 1, half - 1)))
    top_b = jnp.broadcast_to(top.astype(jnp.int16), (tk, tq))

    def bucket_chunk(c, carry):
        hi_ref[c] = jnp.where(hi_ref[c] == top_b, lo_ref[c], jnp.int16(-half))
        return carry

    lax.fori_loop(0, n_live, bucket_chunk, 0)
    low = radix_max(hi_ref, topk - above)
    neg_inf_key = INT_MIN + 0x007FFFFF
    thr = jnp.left_shift(top, 16) + (low + half)
    thr = jnp.where(qpos + 1 <= topk, neg_inf_key + 1, jnp.maximum(thr, neg_inf_key + 1))

    def bias_chunk(c, carry):
        o_ref[0, c] = jnp.where(key_ref[c] >= thr, 0.0, NEG_BIG).astype(o_ref.dtype)
        return carry

    lax.fori_loop(0, n_live, bias_chunk, 0)

    def dead_chunk(c, carry):
        o_ref[0, c] = jnp.full((tk, tq), NEG_BIG, o_ref.dtype)
        return carry

    lax.fori_loop(n_live, n_chunks, dead_chunk, 0)


def dsa_index(hq, qi_col, ki, wi, batch, seq_len):
    tq, tk = IDX_TQ, IDX_TK
    nq, nc = seq_len // tq, seq_len // tk
    topk = min(TOPK_MAX, seq_len // 4)
    return pl.pallas_call(
        functools.partial(_dsa_index_body, topk=topk),
        grid=(batch, nq),
        in_specs=[pl.BlockSpec((tq, IDX_HEADS * IDX_HEAD_DIM), lambda b, i: (b * nq + i, qi_col)),
                  pl.BlockSpec((seq_len, IDX_HEAD_DIM), lambda b, i: (b, 0)),
                  pl.BlockSpec((tq, LANES), lambda b, i: (b * nq + i, 0))],
        out_specs=pl.BlockSpec((1, nc, tk, tq), lambda b, i: (b * nq + i, 0, 0, 0)),
        out_shape=jax.ShapeDtypeStruct((batch * nq, nc, tk, tq), bf16),
        scratch_shapes=[pltpu.VMEM((nc, tk, tq), i32),
                        pltpu.VMEM((nc, tk, tq), jnp.int16),
                        pltpu.VMEM((nc, tk, tq), jnp.int16)],
        compiler_params=_cparams(("parallel", "parallel")),
        name="dsa_index",
    )(hq, ki, wi)


def _dsa_attn_body(qt_ref, ct_ref, q_ref, k_ref, vt_ref, bias_ref, o_ref, m_ref, acc_ref, st_ref, p_ref, bf_ref):
    tq, tk, dh = ATT_TQ, IDX_TK, ATTN_HEAD_DIM
    s = pl.program_id(1)
    i, c = qt_ref[s], ct_ref[s]
    last = ((i + 1) * tq - 1) // tk

    @pl.when(c == 0)
    def _():
        m_ref[...] = jnp.full(m_ref.shape, NEG_BIG, f32)
        acc_ref[...] = jnp.zeros(acc_ref.shape, f32)

    rb = ATT_RB
    for j in range(bias_ref.shape[0]):
        bf_ref[:, j * IDX_TQ:(j + 1) * IDX_TQ] = bias_ref[j, 0].astype(f32)
    pending = None
    for h in range(ATTN_HEADS + 1):
        if h < ATTN_HEADS:
            hs = slice(h * dh, (h + 1) * dh)
            qh = q_ref[:, hs]
            m_old = m_ref[h]
            m_blk = m_old
        for r in range(tk // rb):
            rows = slice(r * rb, (r + 1) * rb)
            if h < ATTN_HEADS:
                st = lax.dot_general(k_ref[rows, hs], qh, (((1,), (1,)), ((), ())),
                                     preferred_element_type=f32) + bf_ref[rows, :]
                st_ref[h, rows, :] = st
                m_blk = jnp.maximum(m_blk, jnp.max(st, axis=0, keepdims=True))
            if pending is not None:
                ph, _, pm = pending
                p_ref[ph, rows, :] = jnp.exp2((st_ref[ph, rows, :] - pm).astype(bf16))
        if pending is not None:
            ph, pm_old, pm = pending
            a = jnp.exp2(pm_old - pm)
            acc_ref[ph] = a * acc_ref[ph] + jnp.dot(vt_ref[ph], p_ref[ph], preferred_element_type=f32)
            m_ref[ph] = pm
        pending = (h, m_old, m_blk) if h < ATTN_HEADS else None

    @pl.when(c == last)
    def _():
        for h in range(ATTN_HEADS):
            acc = acc_ref[h]
            o_ref[:, h * dh:(h + 1) * dh] = (acc[:dh] / acc[dh:dh + 1]).T.astype(o_ref.dtype)


def dsa_attention(hq, q_col, k_col, vt, bias, batch, seq_len):
    tq, tk = ATT_TQ, IDX_TK
    nq, nc = seq_len // tq, seq_len // tk
    sub = tq // IDX_TQ
    steps = [(i, c) for i in range(nq) for c in range(((i + 1) * tq - 1) // tk + 1)]
    q_tab = jnp.asarray(np.array([s[0] for s in steps], np.int32))
    c_tab = jnp.asarray(np.array([s[1] for s in steps], np.int32))
    grid_spec = pltpu.PrefetchScalarGridSpec(
        num_scalar_prefetch=2,
        grid=(batch, len(steps)),
        in_specs=[pl.BlockSpec((tq, ATTN_WIDTH), lambda b, s, qt, ct: (b * nq + qt[s], q_col)),
                  pl.BlockSpec((tk, ATTN_WIDTH), lambda b, s, qt, ct: (b * nc + ct[s], k_col)),
                  pl.BlockSpec((ATTN_HEADS, VT_ROWS, tk), lambda b, s, qt, ct: (0, 0, b * nc + ct[s])),
                  pl.BlockSpec((sub, 1, tk, IDX_TQ), lambda b, s, qt, ct: (b * nq + qt[s], ct[s], 0, 0))],
        out_specs=pl.BlockSpec((tq, ATTN_WIDTH), lambda b, s, qt, ct: (b * nq + qt[s], 0)),
        scratch_shapes=[pltpu.VMEM((ATTN_HEADS, 1, tq), f32),
                        pltpu.VMEM((ATTN_HEADS, VT_ROWS, tq), f32),
                        pltpu.VMEM((ATTN_HEADS, tk, tq), f32),
                        pltpu.VMEM((ATTN_HEADS, tk, tq), bf16),
                        pltpu.VMEM((tk, tq), f32)],
    )
    return pl.pallas_call(
        _dsa_attn_body,
        grid_spec=grid_spec,
        out_shape=jax.ShapeDtypeStruct((batch * seq_len, ATTN_WIDTH), bf16),
        compiler_params=_cparams(("parallel", "arbitrary")),
        name="dsa_attention",
    )(q_tab, c_tab, hq, hq, vt, bias)


def _out_proj_ln_body(a_ref, o_ref, w_ref, x_ref, g_ref, b_ref, y_ref):
    acc = jnp.dot(a_ref[...], w_ref[:POOL_WIDTH, :], preferred_element_type=f32)
    acc = acc + jnp.dot(o_ref[...], w_ref[POOL_WIDTH:, :], preferred_element_type=f32)
    y_ref[...] = _layer_norm(ALPHA * x_ref[...] + acc, g_ref[...], b_ref[...])


def out_proj_ln(a, o, w_out_b, x, g, b):
    m = x.shape[0]
    tm = ROW_TM
    return pl.pallas_call(
        _out_proj_ln_body,
        grid=(m // tm,),
        in_specs=[pl.BlockSpec((tm, POOL_WIDTH), lambda i: (i, 0)),
                  pl.BlockSpec((tm, ATTN_WIDTH), lambda i: (i, 0)),
                  pl.BlockSpec((POOL_WIDTH + ATTN_WIDTH, D_MODEL), lambda i: (0, 0)),
                  pl.BlockSpec((tm, D_MODEL), lambda i: (i, 0)),
                  pl.BlockSpec((1, D_MODEL), lambda i: (0, 0)),
                  pl.BlockSpec((1, D_MODEL), lambda i: (0, 0))],
        out_specs=pl.BlockSpec((tm, D_MODEL), lambda i: (i, 0)),
        out_shape=jax.ShapeDtypeStruct((m, D_MODEL), f32),
        compiler_params=_cparams(("parallel",)),
        name="out_proj_ln",
    )(a, o, w_out_b, x, g.reshape(1, -1), b.reshape(1, -1))


def _conv_out_ln_body(gb_ref, gc_ref, hv_ref, gch_ref, hvh_ref, cw_ref, w_ref, x_ref, g_ref, b_ref, y_ref,
                      *, tiles_per_seq):
    tm = gb_ref.shape[0]
    first = (pl.program_id(0) % tiles_per_seq) == 0
    u = gc_ref[...].astype(f32) * hv_ref[...].astype(f32)
    uh = gch_ref[...].astype(f32) * hvh_ref[...].astype(f32)
    uh = jnp.where(first, 0.0, uh)
    hr = uh.shape[0]
    row = lax.broadcasted_iota(i32, (tm, 1), 0)
    u1 = jnp.where(row == 0, uh[hr - 1:hr, :], pltpu.roll(u, 1, 0))
    u2 = jnp.where(row == 0, uh[hr - 2:hr - 1, :],
                   jnp.where(row == 1, uh[hr - 1:hr, :], pltpu.roll(u, 2, 0)))
    z = cw_ref[0:1, :] * u2 + cw_ref[1:2, :] * u1 + cw_ref[2:3, :] * u
    gz = (gb_ref[...].astype(f32) * z).astype(bf16)
    acc = jnp.dot(gz, w_ref[...], preferred_element_type=f32)
    y_ref[...] = _layer_norm(ALPHA * x_ref[...] + acc, g_ref[...], b_ref[...])


def conv_out_ln(h1, conv_w, w_out_b, x, g, b, seq_len):
    m = x.shape[0]
    tm = ROW_TM
    halo = 16
    hb = tm // halo

    def halo_idx(col):
        return lambda i: (jnp.maximum(i * hb - 1, 0), col)

    return pl.pallas_call(
        functools.partial(_conv_out_ln_body, tiles_per_seq=seq_len // tm),
        grid=(m // tm,),
        in_specs=[pl.BlockSpec((tm, D_MODEL), lambda i: (i, 0)),
                  pl.BlockSpec((tm, D_MODEL), lambda i: (i, 1)),
                  pl.BlockSpec((tm, D_MODEL), lambda i: (i, 2)),
                  pl.BlockSpec((halo, D_MODEL), halo_idx(1)),
                  pl.BlockSpec((halo, D_MODEL), halo_idx(2)),
                  pl.BlockSpec((CONV_WIDTH, D_MODEL), lambda i: (0, 0)),
                  pl.BlockSpec((D_MODEL, D_MODEL), lambda i: (0, 0)),
                  pl.BlockSpec((tm, D_MODEL), lambda i: (i, 0)),
                  pl.BlockSpec((1, D_MODEL), lambda i: (0, 0)),
                  pl.BlockSpec((1, D_MODEL), lambda i: (0, 0))],
        out_specs=pl.BlockSpec((tm, D_MODEL), lambda i: (i, 0)),
        out_shape=jax.ShapeDtypeStruct((m, D_MODEL), f32),
        compiler_params=_cparams(("parallel",)),
        name="conv_out_ln",
    )(h1, h1, h1, h1, h1, conv_w, w_out_b, x, g.reshape(1, -1), b.reshape(1, -1))


ROUTE_E0, ROUTE_E1, ROUTE_C0, ROUTE_C1, ROUTE_R0, ROUTE_R1 = range(6)


def _split_bf16(x):
    hi = x.astype(bf16)
    lo = (x - hi.astype(f32)).astype(bf16)
    return hi, lo


def _router_body(x_ref, w_ref, b_ref, o_ref, cnt_ref):
    i = pl.program_id(0)

    @pl.when(i == 0)
    def _():
        cnt_ref[...] = jnp.zeros(cnt_ref.shape, f32)

    xh, xl = _split_bf16(x_ref[...])
    wh, wl = _split_bf16(w_ref[...])
    lg = (jnp.dot(xh, wh, preferred_element_type=f32) + jnp.dot(xh, wl, preferred_element_type=f32)
          + jnp.dot(xl, wh, preferred_element_type=f32)) + b_ref[...]
    tm, nl = lg.shape
    lane = lax.broadcasted_iota(i32, (tm, nl), 1)
    neg = -jnp.inf
    is_g = lane < N_GROUPS
    gl = jnp.where(is_g, lg, neg)
    gmax = jnp.max(gl, axis=1, keepdims=True)
    gidx = jnp.min(jnp.where(gl == gmax, lane, nl), axis=1, keepdims=True)
    gw = 1.0 / jnp.sum(jnp.where(is_g, jnp.exp(lg - gmax), 0.0), axis=1, keepdims=True)
    e_lo = N_GROUPS + gidx * EXPERTS_PER_GROUP
    in_grp = (lane >= e_lo) & (lane < e_lo + EXPERTS_PER_GROUP)
    el = jnp.where(in_grp, lg, neg)
    l0 = jnp.max(el, axis=1, keepdims=True)
    i0 = jnp.min(jnp.where(el == l0, lane, nl), axis=1, keepdims=True)
    el1 = jnp.where(lane == i0, neg, el)
    l1 = jnp.max(el1, axis=1, keepdims=True)
    i1 = jnp.min(jnp.where(el1 == l1, lane, nl), axis=1, keepdims=True)
    p1 = jnp.exp(l1 - l0)
    c0 = gw / (1.0 + p1)
    c1 = gw * p1 / (1.0 + p1)
    hot0, hot1 = lane == i0, lane == i1
    onehot = jnp.where(hot0, 1.0, jnp.where(hot1, 1.0, 0.0))
    before = lax.broadcasted_iota(i32, (tm, tm), 0) > lax.broadcasted_iota(i32, (tm, tm), 1)
    seen = jnp.dot(jnp.where(before, 1.0, 0.0).astype(bf16), onehot.astype(bf16),
                   preferred_element_type=f32) + cnt_ref[...]
    r0 = jnp.sum(jnp.where(hot0, seen, 0.0), axis=1, keepdims=True)
    r1 = jnp.sum(jnp.where(hot1, seen, 0.0), axis=1, keepdims=True)
    cnt_ref[...] += jnp.sum(onehot, axis=0, keepdims=True)
    cols = {ROUTE_E0: (i0 - N_GROUPS).astype(f32), ROUTE_E1: (i1 - N_GROUPS).astype(f32),
            ROUTE_C0: c0, ROUTE_C1: c1, ROUTE_R0: r0, ROUTE_R1: r1}
    out = jnp.zeros((tm, nl), f32)
    for k, v in cols.items():
        out = jnp.where(lane == k, v, out)
    o_ref[...] = out


def moe_router(x, w_group, b_group, w_expert, b_expert):
    m = x.shape[0]
    tm = ROW_TM
    nl = LANES
    w = jnp.zeros((D_MODEL, nl), f32).at[:, :N_GROUPS].set(w_group).at[:, N_GROUPS:N_GROUPS + N_EXPERTS].set(w_expert)
    bias = jnp.zeros((1, nl), f32).at[0, :N_GROUPS].set(b_group).at[0, N_GROUPS:N_GROUPS + N_EXPERTS].set(b_expert)
    return pl.pallas_call(
        _router_body,
        grid=(m // tm,),
        in_specs=[pl.BlockSpec((tm, D_MODEL), lambda i: (i, 0)),
                  pl.BlockSpec((D_MODEL, nl), lambda i: (0, 0)),
                  pl.BlockSpec((1, nl), lambda i: (0, 0))],
        out_specs=[pl.BlockSpec((tm, nl), lambda i: (i, 0)),
                   pl.BlockSpec((1, nl), lambda i: (0, 0))],
        out_shape=[jax.ShapeDtypeStruct((m, nl), f32),
                   jax.ShapeDtypeStruct((1, nl), f32)],
        compiler_params=_cparams(("arbitrary",)),
        name="moe_router",
    )(x, w, bias)


def _route_plan(route, counts_row, tm, n_tiles):
    counts = counts_row[0, N_GROUPS:N_GROUPS + N_EXPERTS].astype(i32)
    tiles_per = (counts + tm - 1) // tm
    tile_end = jnp.cumsum(tiles_per)
    tile_start = tile_end - tiles_per
    ri = route[:, :8].astype(i32)
    experts = jnp.arange(N_EXPERTS, dtype=i32)[None, :]

    def slot(e, r):
        return jnp.sum(jnp.where(e[:, None] == experts, tile_start[None, :], 0), axis=1) * tm + r

    pos0 = slot(ri[:, ROUTE_E0], ri[:, ROUTE_R0])
    pos1 = slot(ri[:, ROUTE_E1], ri[:, ROUTE_R1])
    tile_ids = jnp.arange(n_tiles, dtype=i32)
    n_valid = tile_end[-1]
    valid = tile_ids < n_valid
    tile_e = jnp.sum((tile_ids[:, None] >= tile_end[None, :]).astype(i32), axis=1)
    last_e = jnp.sum((n_valid - 1 >= tile_end).astype(i32))
    tile_e = jnp.where(valid, tile_e, last_e).astype(i32)
    prev_e = jnp.concatenate([jnp.full((1,), -1, i32), tile_e[:-1]])
    first = ((tile_e != prev_e) & valid).astype(i32)
    pad_lo = tile_start * tm + counts
    pad_hi = tile_end * tm
    used = tiles_per > 0
    e_ids = jnp.arange(N_EXPERTS, dtype=i32)
    slot_of_e = (jnp.cumsum(used.astype(i32)) - 1) % 2
    later_used = (e_ids[None, :] > e_ids[:, None]) & used[None, :]
    next_of_e = jnp.min(jnp.where(later_used, e_ids[None, :], N_EXPERTS), axis=1)
    next_of_e = jnp.where(next_of_e == N_EXPERTS, -1, next_of_e)
    pick = tile_e[:, None] == e_ids[None, :]
    w_slot = jnp.sum(jnp.where(pick, slot_of_e[None, :], 0), axis=1).astype(i32)
    w_next = jnp.sum(jnp.where(pick, next_of_e[None, :], 0), axis=1).astype(i32)
    return pos0, pos1, pad_lo, pad_hi, tile_e, first, valid.astype(i32), w_slot, w_next


def _start_row_gathers(src_hbm, dst_ref, sem, rows, row_of):
    def body(j, carry):
        for k in range(DMA_UNROLL):
            r = j * DMA_UNROLL + k
            pltpu.make_async_copy(src_hbm.at[pl.ds(row_of(r), 1), :], dst_ref.at[pl.ds(r, 1), :], sem).start()
        return carry
    lax.fori_loop(0, rows // DMA_UNROLL, body, 0)


def _wait_row_gathers(src_hbm, dst_ref, sem):
    rows = dst_ref.shape[0]
    pltpu.make_async_copy(src_hbm.at[pl.ds(0, rows), :], dst_ref, sem).wait()


def _moe_ffn_body(p0_ref, p1_ref, lo_ref, hi_ref, te_ref, first_ref, valid_ref, wslot_ref, wnext_ref,
                  x_hbm, wg_hbm, wu_hbm, wd_hbm, y_ref,
                  src_ref, xg_ref, sem_ref, wst_g, wst_u, wst_d, wsem_ref, wgb_ref, wub_ref, wdb_ref, *, layer):
    tm = MOE_TM
    t = pl.program_id(0)
    slot = t % 2
    n_tok = p0_ref.shape[0]
    is_valid = valid_ref[t] == 1
    staged = ((wg_hbm, wst_g, wgb_ref), (wu_hbm, wst_u, wub_ref), (wd_hbm, wst_d, wdb_ref))

    def weight_copy(k, expert, ws):
        hbm, stage, _ = staged[k]
        return pltpu.make_async_copy(hbm.at[layer, expert], stage.at[ws], wsem_ref.at[ws, k])

    @pl.when(t == 0)
    def _():
        for k in range(3):
            weight_copy(k, te_ref[0], wslot_ref[0]).start()
        def fill(n, carry):
            src_ref[p0_ref[n]] = n
            src_ref[p1_ref[n]] = n
            return carry
        lax.fori_loop(0, n_tok, fill, 0, unroll=8)

        def pad(p, carry):
            src_ref[p] = 0
            return carry
        for e in range(N_EXPERTS):
            lax.fori_loop(lo_ref[e], hi_ref[e], pad, 0)
        end = hi_ref[N_EXPERTS - 1]
        lax.fori_loop(end, end + tm, pad, 0)
        _start_row_gathers(x_hbm, xg_ref.at[0], sem_ref.at[0], tm, lambda r: src_ref[r])

    @pl.when(first_ref[t] == 1)
    def _():
        ws = wslot_ref[t]
        for k in range(3):
            weight_copy(k, te_ref[t], ws).wait()
            staged[k][2][...] = staged[k][1][ws].astype(bf16)

        @pl.when(wnext_ref[t] >= 0)
        def _():
            for k in range(3):
                weight_copy(k, wnext_ref[t], 1 - ws).start()

    @pl.when(is_valid)
    def _():
        _wait_row_gathers(x_hbm, xg_ref.at[slot], sem_ref.at[slot])
        nxt = (t + 1) * tm

        def prefetch(lo, hi):
            for r in range(lo, hi):
                pltpu.make_async_copy(x_hbm.at[pl.ds(src_ref[nxt + r], 1), :],
                                      xg_ref.at[1 - slot, pl.ds(r, 1), :], sem_ref.at[1 - slot]).start()

        xb = xg_ref[slot].astype(bf16)
        gate = jnp.dot(xb, wgb_ref[...], preferred_element_type=f32)
        prefetch(0, tm // 2)
        up = jnp.dot(xb, wub_ref[...], preferred_element_type=f32)
        prefetch(tm // 2, tm)
        hdn = (gate * jax.nn.sigmoid(gate) * up).astype(bf16)
        y_ref[...] = jnp.dot(hdn, wdb_ref[...], preferred_element_type=f32)

    @pl.when(jnp.logical_not(is_valid))
    def _():
        @pl.when(valid_ref[jnp.maximum(t - 1, 0)] == 1)
        def _():
            _wait_row_gathers(x_hbm, xg_ref.at[slot], sem_ref.at[slot])
        y_ref[...] = jnp.zeros(y_ref.shape, y_ref.dtype)


def moe_ffn(x, plan, w_gate, w_up, w_down, layer, n_tiles):
    tm = MOE_TM
    grid_spec = pltpu.PrefetchScalarGridSpec(
        num_scalar_prefetch=len(plan),
        grid=(n_tiles,),
        in_specs=[pl.BlockSpec(memory_space=pl.ANY)] * 4,
        out_specs=pl.BlockSpec((tm, D_MODEL), lambda t, *pre: (t, 0)),
        scratch_shapes=[pltpu.SMEM((n_tiles * tm,), i32),
                        pltpu.VMEM((2, tm, D_MODEL), f32),
                        pltpu.SemaphoreType.DMA((2,)),
                        pltpu.VMEM((2, D_MODEL, D_FF_EXPERT), f32),
                        pltpu.VMEM((2, D_MODEL, D_FF_EXPERT), f32),
                        pltpu.VMEM((2, D_FF_EXPERT, D_MODEL), f32),
                        pltpu.SemaphoreType.DMA((2, 3)),
                        pltpu.VMEM((D_MODEL, D_FF_EXPERT), bf16),
                        pltpu.VMEM((D_MODEL, D_FF_EXPERT), bf16),
                        pltpu.VMEM((D_FF_EXPERT, D_MODEL), bf16)],
    )
    return pl.pallas_call(
        functools.partial(_moe_ffn_body, layer=layer),
        grid_spec=grid_spec,
        out_shape=jax.ShapeDtypeStruct((n_tiles * tm, D_MODEL), f32),
        compiler_params=_cparams(("arbitrary",)),
        name="moe_ffn",
    )(*plan, x, w_gate, w_up, w_down)


def _combine_ln_body(p0_ref, p1_ref, y_hbm, x_ref, r_ref, g_ref, b_ref, *rest, with_bf16):
    o_ref = rest[0]
    yg_ref, sem_ref = rest[-2:]
    tm = ROW_TM
    t = pl.program_id(0)
    nt = pl.num_programs(0)
    slot = t % 2
    p_refs = (p0_ref, p1_ref)

    @pl.when(t == 0)
    def _():
        for j in range(2):
            _start_row_gathers(y_hbm, yg_ref.at[0, j], sem_ref.at[0, j], tm, lambda r: p_refs[j][r])

    for j in range(2):
        _wait_row_gathers(y_hbm, yg_ref.at[slot, j], sem_ref.at[slot, j])

    nxt = jnp.minimum(t + 1, nt - 1) * tm

    def prefetch(j, lo, hi):
        for r in range(lo, hi):
            pltpu.make_async_copy(y_hbm.at[pl.ds(p_refs[j][nxt + r], 1), :],
                                  yg_ref.at[1 - slot, j, pl.ds(r, 1), :], sem_ref.at[1 - slot, j]).start()

    r = r_ref[...]
    prefetch(0, 0, tm // 2)
    f = r[:, ROUTE_C0:ROUTE_C0 + 1] * yg_ref[slot, 0] + r[:, ROUTE_C1:ROUTE_C1 + 1] * yg_ref[slot, 1]
    prefetch(0, tm // 2, tm)
    z = ALPHA * x_ref[...] + f
    prefetch(1, 0, tm // 2)
    y = _layer_norm(z, g_ref[...], b_ref[...])
    prefetch(1, tm // 2, tm)
    o_ref[...] = y
    if with_bf16:
        rest[1][...] = y.astype(bf16)

    @pl.when(t == nt - 1)
    def _():
        for j in range(2):
            _wait_row_gathers(y_hbm, yg_ref.at[1 - slot, j], sem_ref.at[1 - slot, j])


def combine_ln(y_sorted, pos0, pos1, route, x, g, b, with_bf16):
    m = x.shape[0]
    tm = ROW_TM
    row = lambda t, p0, p1: (t, 0)
    fixed = lambda t, p0, p1: (0, 0)
    out_specs = [pl.BlockSpec((tm, D_MODEL), row)]
    out_shape = [jax.ShapeDtypeStruct((m, D_MODEL), f32)]
    if with_bf16:
        out_specs.append(pl.BlockSpec((tm, D_MODEL), row))
        out_shape.append(jax.ShapeDtypeStruct((m, D_MODEL), bf16))
    grid_spec = pltpu.PrefetchScalarGridSpec(
        num_scalar_prefetch=2,
        grid=(m // tm,),
        in_specs=[pl.BlockSpec(memory_space=pl.ANY),
                  pl.BlockSpec((tm, D_MODEL), row),
                  pl.BlockSpec((tm, LANES), row),
                  pl.BlockSpec((1, D_MODEL), fixed),
                  pl.BlockSpec((1, D_MODEL), fixed)],
        out_specs=out_specs,
        scratch_shapes=[pltpu.VMEM((2, 2, tm, D_MODEL), f32),
                        pltpu.SemaphoreType.DMA((2, 2))],
    )
    return pl.pallas_call(
        functools.partial(_combine_ln_body, with_bf16=with_bf16),
        grid_spec=grid_spec,
        out_shape=out_shape,
        compiler_params=_cparams(("arbitrary",)),
        name="moe_combine_ln",
    )(pos0, pos1, y_sorted, x, route, g.reshape(1, -1), b.reshape(1, -1))


def hier_moe_ln(x, w_group, b_group, w_expert, b_expert, w_gate, w_up, w_down, layer, g, b, with_bf16):
    n = x.shape[0]
    n_tiles = 2 * n // MOE_TM + N_EXPERTS
    route, counts = moe_router(x, w_group, b_group, w_expert, b_expert)
    plan = _route_plan(route, counts, MOE_TM, n_tiles)
    y_sorted = moe_ffn(x, plan, w_gate, w_up, w_down, layer, n_tiles)
    return combine_ln(y_sorted, plan[0], plan[1], route, x, g, b, with_bf16)


def kernel(x, ab_w_in, ab_idx_k_ln_g, ab_idx_k_ln_b, ab_pool_w, ab_pool_scale, ab_w_out, c_w_in, c_conv_w, c_w_out,
           ln_mix_g, ln_mix_b, ln_ffn_g, ln_ffn_b, moe_w_group, moe_b_group, moe_w_expert, moe_b_expert,
           moe_w_gate, moe_w_up, moe_w_down):
    batch, seq_len, d = x.shape
    n = batch * seq_len
    xf = x.reshape(n, d)
    moe = lambda layer: (moe_w_group[layer], moe_b_group[layer], moe_w_expert[layer], moe_b_expert[layer],
                         moe_w_gate, moe_w_up, moe_w_down, layer)

    xb = cast_bf16(xf)
    w_in = ab_w_in[0]
    w_in_b = cast_bf16(w_in)
    hq = matmul_bf16(xb, w_in_b, (0, 1, 2, 4), scales=(1.0, ATTN_HEAD_DIM ** -0.5 * LOG2E, 1.0, 1.0))
    vt = value_t(cast_bf16(w_in[:, V_COL * COL:(V_COL + 1) * COL].T), xb)
    tail = 5 * COL
    w_wi = jnp.pad(w_in[:, tail + IDX_HEAD_DIM:], ((0, 0), (0, LANES - IDX_HEADS)))
    ki, wi = idx_proj(xb, w_in[:, tail:tail + IDX_HEAD_DIM], w_wi, ab_idx_k_ln_g[0], ab_idx_k_ln_b[0])
    a = pool_mixer(hq, ab_pool_w[0], ab_pool_scale[0], seq_len)
    bias = dsa_index(hq, 3, ki, wi, batch, seq_len)
    o = dsa_attention(hq, 1, 2, vt, bias, batch, seq_len)
    x1 = out_proj_ln(a, o, cast_bf16(ab_w_out[0]), xf, ln_mix_g[0], ln_mix_b[0])
    x2, x2b = hier_moe_ln(x1, *moe(0), ln_ffn_g[0], ln_ffn_b[0], True)

    h1 = matmul_bf16(x2b, cast_bf16(c_w_in[0]), tuple(range(3 * D_MODEL // MM_TN)))
    x3 = conv_out_ln(h1, c_conv_w[0], cast_bf16(c_w_out[0]), x2, ln_mix_g[1], ln_mix_b[1], seq_len)
    (x4,) = hier_moe_ln(x3, *moe(1), ln_ffn_g[1], ln_ffn_b[1], False)
    return x4.reshape(batch, seq_len, d)
```

```python
import functools
import math

import numpy as np
import jax
import jax.numpy as jnp
from jax import lax
from jax.experimental import pallas as pl
from jax.experimental.pallas import tpu as pltpu

f32 = jnp.float32
bf16 = jnp.bfloat16
i32 = jnp.int32

D_MODEL = 2048
POOL_WINDOWS = (2, 4, 8, 16)
POOL_WIDTH = 1024
POOL_GROUP = 256
ATTN_HEADS = 8
ATTN_HEAD_DIM = 128
ATTN_WIDTH = 1024
IDX_HEADS = 16
IDX_HEAD_DIM = 64
TOPK_MAX = 256
CONV_WIDTH = 3
N_GROUPS = 4
EXPERTS_PER_GROUP = 8
N_EXPERTS = 32
D_FF_EXPERT = 512
DEPTH = 2
ALPHA = (2 * DEPTH) ** 0.25
LN_EPS = 1e-5
COL = 1024
V_COL = 3

LANES = 128
VMEM_LIMIT = 56 * 1024 * 1024
MM_TM, MM_TN = 512, 1024
ROW_TM = 256
IDX_TQ, IDX_TK = 256, 512
IDX_KS = 128
ATT_TQ = 256
ATT_RB = 128
VT_ROWS = ATTN_HEAD_DIM + 16
MOE_TM = 256
DMA_UNROLL = 8
GATHER_SLOTS = 3
NEG_BIG = -1e30
INT_MIN = -(2 ** 31)
LOG2E = math.log2(math.e)


def _cparams(sem):
    return pltpu.CompilerParams(dimension_semantics=sem, vmem_limit_bytes=VMEM_LIMIT)


def _layer_norm(z, g, b):
    mu = jnp.mean(z, axis=-1, keepdims=True)
    zc = z - mu
    var = jnp.mean(zc * zc, axis=-1, keepdims=True)
    return zc * lax.rsqrt(var + LN_EPS) * g + b


def _cast_body(x_ref, o_ref):
    o_ref[...] = x_ref[...].astype(o_ref.dtype)


def cast_bf16(x, tm=512):
    m, n = x.shape
    return pl.pallas_call(
        _cast_body,
        grid=(m // tm,),
        in_specs=[pl.BlockSpec((tm, n), lambda i: (i, 0))],
        out_specs=pl.BlockSpec((tm, n), lambda i: (i, 0)),
        out_shape=jax.ShapeDtypeStruct((m, n), bf16),
        compiler_params=_cparams(("parallel",)),
        name="cast_bf16",
    )(x)


def _mm_body(x_ref, w_ref, o_ref, *, scales):
    acc = jnp.dot(x_ref[...], w_ref[...], preferred_element_type=f32)
    j = pl.program_id(0)
    scale = jnp.float32(1.0)
    for jj, s in enumerate(scales):
        if s != 1.0:
            scale = jnp.where(j == jj, jnp.float32(s), scale)
    o_ref[...] = (acc * scale).astype(o_ref.dtype)


def matmul_bf16(x, w, col_blocks, scales=None):
    m, k = x.shape
    nb = len(col_blocks)
    scales = tuple(scales) if scales is not None else (1.0,) * nb
    skip = [c for c in range(col_blocks[0], col_blocks[-1] + 1) if c not in col_blocks]
    assert len(skip) <= 1 and list(col_blocks) == sorted(col_blocks)
    first = col_blocks[0]

    def w_col(j):
        c = j + first
        return c + (c >= skip[0]).astype(i32) if skip else c

    return pl.pallas_call(
        functools.partial(_mm_body, scales=scales),
        grid=(nb, m // MM_TM),
        in_specs=[pl.BlockSpec((MM_TM, k), lambda j, i: (i, 0)),
                  pl.BlockSpec((k, MM_TN), lambda j, i: (0, w_col(j)))],
        out_specs=pl.BlockSpec((MM_TM, MM_TN), lambda j, i: (i, j)),
        out_shape=jax.ShapeDtypeStruct((m, nb * MM_TN), bf16),
        compiler_params=_cparams(("parallel", "parallel")),
        name="proj_matmul",
    )(x, w)


def _value_t_body(wt_ref, x_ref, o_ref):
    vt = lax.dot_general(wt_ref[...], x_ref[...], (((1,), (1,)), ((), ())), preferred_element_type=f32)
    dh = ATTN_HEAD_DIM
    for h in range(ATTN_HEADS):
        o_ref[h, :dh, :] = vt[h * dh:(h + 1) * dh, :].astype(o_ref.dtype)
        o_ref[h, dh:, :] = jnp.ones((VT_ROWS - dh, o_ref.shape[2]), o_ref.dtype)


def value_t(wt, x):
    c, k = wt.shape
    m = x.shape[0]
    return pl.pallas_call(
        _value_t_body,
        grid=(m // MM_TM,),
        in_specs=[pl.BlockSpec((c, k), lambda i: (0, 0)),
                  pl.BlockSpec((MM_TM, k), lambda i: (i, 0))],
        out_specs=pl.BlockSpec((ATTN_HEADS, VT_ROWS, MM_TM), lambda i: (0, 0, i)),
        out_shape=jax.ShapeDtypeStruct((ATTN_HEADS, VT_ROWS, m), bf16),
        compiler_params=_cparams(("parallel",)),
        name="value_t",
    )(wt, x)


def _idx_proj_body(x_ref, wk_ref, ww_ref, g_ref, b_ref, ki_ref, wi_ref, *, wi_scale):
    x = x_ref[...]
    hk = jnp.dot(x, wk_ref[...].astype(bf16), preferred_element_type=f32)
    hw = jnp.dot(x, ww_ref[...].astype(bf16), preferred_element_type=f32)
    ki_ref[...] = _layer_norm(hk, g_ref[...], b_ref[...]).astype(ki_ref.dtype)
    wi_ref[...] = hw * wi_scale


def idx_proj(xb, w_ki, w_wi, g, b):
    m, k = xb.shape
    tm = MM_TM
    wi_scale = (IDX_HEADS ** -0.5) * (IDX_HEAD_DIM ** -0.5)
    return pl.pallas_call(
        functools.partial(_idx_proj_body, wi_scale=wi_scale),
        grid=(m // tm,),
        in_specs=[pl.BlockSpec((tm, k), lambda i: (i, 0)),
                  pl.BlockSpec((k, IDX_HEAD_DIM), lambda i: (0, 0)),
                  pl.BlockSpec((k, LANES), lambda i: (0, 0)),
                  pl.BlockSpec((1, IDX_HEAD_DIM), lambda i: (0, 0)),
                  pl.BlockSpec((1, IDX_HEAD_DIM), lambda i: (0, 0))],
        out_specs=[pl.BlockSpec((tm, IDX_HEAD_DIM), lambda i: (i, 0)),
                   pl.BlockSpec((tm, LANES), lambda i: (i, 0))],
        out_shape=[jax.ShapeDtypeStruct((m, IDX_HEAD_DIM), bf16),
                   jax.ShapeDtypeStruct((m, LANES), f32)],
        compiler_params=_cparams(("parallel",)),
        name="idx_proj",
    )(xb, w_ki, w_wi, g.reshape(1, -1), b.reshape(1, -1))


def _pool_body(up_ref, uc_ref, pw_ref, ps_ref, o_ref, *, tiles_per_seq):
    tm = uc_ref.shape[0]
    i = pl.program_id(0)
    t0 = (i % tiles_per_seq) * tm
    row = lax.broadcasted_iota(i32, (tm, 2 * tm), 0)
    col = lax.broadcasted_iota(i32, (tm, 2 * tm), 1) - tm
    lo_ok = col + t0 >= 0
    pos = t0 + lax.broadcasted_iota(i32, (tm, 1), 0)
    for g, win in enumerate(POOL_WINDOWS):
        cs = slice(g * POOL_GROUP, (g + 1) * POOL_GROUP)
        band = jnp.where((col <= row) & (col > row - win) & lo_ok, 1.0, 0.0).astype(bf16)
        uc = uc_ref[:, cs]
        ucat = jnp.concatenate([up_ref[:, cs], uc], axis=0)
        ssum = jnp.dot(band, ucat, preferred_element_type=f32)
        cnt = jnp.minimum(pos + 1, win).astype(f32)
        pooled = ssum / cnt - uc.astype(f32)
        a = jnp.dot(pooled.astype(bf16), pw_ref[g].astype(bf16), preferred_element_type=f32)
        o_ref[:, cs] = (a * ps_ref[:, cs]).astype(o_ref.dtype)


def pool_mixer(h, pool_w, pool_scale, seq_len):
    m = h.shape[0]
    tm = ROW_TM
    return pl.pallas_call(
        functools.partial(_pool_body, tiles_per_seq=seq_len // tm),
        grid=(m // tm,),
        in_specs=[pl.BlockSpec((tm, POOL_WIDTH), lambda i: (jnp.maximum(i - 1, 0), 0)),
                  pl.BlockSpec((tm, POOL_WIDTH), lambda i: (i, 0)),
                  pl.BlockSpec((len(POOL_WINDOWS), POOL_GROUP, POOL_GROUP), lambda i: (0, 0, 0)),
                  pl.BlockSpec((1, POOL_WIDTH), lambda i: (0, 0))],
        out_specs=pl.BlockSpec((tm, POOL_WIDTH), lambda i: (i, 0)),
        out_shape=jax.ShapeDtypeStruct((m, POOL_WIDTH), bf16),
        compiler_params=_cparams(("parallel",)),
        name="pool_mixer",
    )(h, h, pool_w, pool_scale.reshape(1, -1))


def _sortable(x):
    bits = pltpu.bitcast(x, i32)
    return bits ^ ((bits >> 31) & 0x7FFFFFFF)


def _dsa_index_body(qi_ref, ki_ref, wi_ref, o_ref, key_ref, hi_ref, lo_ref, *, topk):
    tq, tk, ks = IDX_TQ, IDX_TK, IDX_KS
    n_chunks = key_ref.shape[0]
    q0 = pl.program_id(1) * tq
    n_live = (q0 + tq + tk - 1) // tk
    wt = wi_ref[...].T
    qpos = q0 + lax.broadcasted_iota(i32, (1, tq), 1)
    krow = lax.broadcasted_iota(i32, (ks, 1), 0)

    def score_chunk(c, carry):
        k0 = pl.multiple_of(c * tk, tk)
        for s in range(tk // ks):
            kic = ki_ref[pl.ds(k0 + s * ks, ks), :]
            acc = jnp.zeros((ks, tq), f32)
            for h in range(IDX_HEADS):
                qh = qi_ref[:, h * IDX_HEAD_DIM:(h + 1) * IDX_HEAD_DIM]
                sc = lax.dot_general(kic, qh, (((1,), (1,)), ((), ())), preferred_element_type=f32)
                acc = acc + wt[h:h + 1, :] * jnp.maximum(sc, 0.0)
            acc = jnp.where(krow + (k0 + s * ks) <= qpos, acc, -jnp.inf)
            key_ref[c, s * ks:(s + 1) * ks, :] = _sortable(acc)
        return carry

    lax.fori_loop(0, n_live, score_chunk, 0)

    psub = 16
    n_part = 4
    half = 1 << 15

    def count_ge(ref, test):
        test_b = jnp.broadcast_to(test.astype(jnp.int16), (psub, tq))

        def count_chunk(c, parts):
            parts = list(parts)
            for r in range(tk // psub):
                kk = ref[c, r * psub:(r + 1) * psub, :]
                parts[r % n_part] = parts[r % n_part] + jnp.where(kk >= test_b, jnp.int16(1), jnp.int16(0))
            return tuple(parts)

        parts = lax.fori_loop(0, n_live, count_chunk, tuple(jnp.zeros((psub, tq), jnp.int16) for _ in range(n_part)))
        tot = parts[0].astype(i32) + parts[1].astype(i32) + parts[2].astype(i32) + parts[3].astype(i32)
        return jnp.sum(tot, axis=0, keepdims=True)

    def radix_max(ref, need):
        def bit_step(b, cand):
            test = cand | jnp.left_shift(jnp.int32(1), 15 - b)
            cnt = count_ge(ref, test - half)
            return jnp.where(cnt >= need, test, cand)
        return lax.fori_loop(0, 16, bit_step, jnp.zeros((1, tq), i32)) - half

    def split_chunk(c, carry):
        kk = key_ref[c]
        hi_ref[c] = (kk >> 16).astype(jnp.int16)
        lo_ref[c] = ((kk & 0xFFFF) - half).astype(jnp.int16)
        return carry

    lax.fori_loop(0, n_live, split_chunk, 0)
    top = radix_max(hi_ref, topk)
    above = jnp.where(top == half - 1, 0, count_ge(hi_ref, jnp.minimum(top + 1, half - 1)))
    top_b = jnp.broadcast_to(top.astype(jnp.int16), (tk, tq))

    def bucket_chunk(c, carry):
        hi_ref[c] = jnp.where(hi_ref[c] == top_b, lo_ref[c], jnp.int16(-half))
        return carry

    lax.fori_loop(0, n_live, bucket_chunk, 0)
    low = radix_max(hi_ref, topk - above)
    neg_inf_key = INT_MIN + 0x007FFFFF
    thr = jnp.left_shift(top, 16) + (low + half)
    thr = jnp.where(qpos + 1 <= topk, neg_inf_key + 1, jnp.maximum(thr, neg_inf_key + 1))

    def bias_chunk(c, carry):
        o_ref[0, c] = jnp.where(key_ref[c] >= thr, 0.0, NEG_BIG).astype(o_ref.dtype)
        return carry

    lax.fori_loop(0, n_live, bias_chunk, 0)

    def dead_chunk(c, carry):
        o_ref[0, c] = jnp.full((tk, tq), NEG_BIG, o_ref.dtype)
        return carry

    lax.fori_loop(n_live, n_chunks, dead_chunk, 0)


def dsa_index(hq, qi_col, ki, wi, batch, seq_len):
    tq, tk = IDX_TQ, IDX_TK
    nq, nc = seq_len // tq, seq_len // tk
    topk = min(TOPK_MAX, seq_len // 4)
    return pl.pallas_call(
        functools.partial(_dsa_index_body, topk=topk),
        grid=(batch, nq),
        in_specs=[pl.BlockSpec((tq, IDX_HEADS * IDX_HEAD_DIM), lambda b, i: (b * nq + i, qi_col)),
                  pl.BlockSpec((seq_len, IDX_HEAD_DIM), lambda b, i: (b, 0)),
                  pl.BlockSpec((tq, LANES), lambda b, i: (b * nq + i, 0))],
        out_specs=pl.BlockSpec((1, nc, tk, tq), lambda b, i: (b * nq + i, 0, 0, 0)),
        out_shape=jax.ShapeDtypeStruct((batch * nq, nc, tk, tq), bf16),
        scratch_shapes=[pltpu.VMEM((nc, tk, tq), i32),
                        pltpu.VMEM((nc, tk, tq), jnp.int16),
                        pltpu.VMEM((nc, tk, tq), jnp.int16)],
        compiler_params=_cparams(("parallel", "parallel")),
        name="dsa_index",
    )(hq, ki, wi)


def _dsa_attn_body(qt_ref, ct_ref, q_ref, k_ref, vt_ref, bias_ref, o_ref, m_ref, acc_ref, st_ref, p_ref, bf_ref):
    tq, tk, dh = ATT_TQ, IDX_TK, ATTN_HEAD_DIM
    s = pl.program_id(1)
    i, c = qt_ref[s], ct_ref[s]
    last = ((i + 1) * tq - 1) // tk

    @pl.when(c == 0)
    def _():
        m_ref[...] = jnp.full(m_ref.shape, NEG_BIG, f32)
        acc_ref[...] = jnp.zeros(acc_ref.shape, f32)

    rb = ATT_RB
    for j in range(bias_ref.shape[0]):
        bf_ref[:, j * IDX_TQ:(j + 1) * IDX_TQ] = bias_ref[j, 0].astype(f32)
    pending = None
    for h in range(ATTN_HEADS + 1):
        if h < ATTN_HEADS:
            hs = slice(h * dh, (h + 1) * dh)
            qh = q_ref[:, hs]
            m_old = m_ref[h]
            m_blk = m_old
        for r in range(tk // rb):
            rows = slice(r * rb, (r + 1) * rb)
            if h < ATTN_HEADS:
                st = lax.dot_general(k_ref[rows, hs], qh, (((1,), (1,)), ((), ())),
                                     preferred_element_type=f32) + bf_ref[rows, :]
                st_ref[h, rows, :] = st
                m_blk = jnp.maximum(m_blk, jnp.max(st, axis=0, keepdims=True))
            if pending is not None:
                ph, _, pm = pending
                p_ref[ph, rows, :] = jnp.exp2((st_ref[ph, rows, :] - pm).astype(bf16))
        if pending is not None:
            ph, pm_old, pm = pending
            a = jnp.exp2(pm_old - pm)
            acc_ref[ph] = a * acc_ref[ph] + jnp.dot(vt_ref[ph], p_ref[ph], preferred_element_type=f32)
            m_ref[ph] = pm
        pending = (h, m_old, m_blk) if h < ATTN_HEADS else None

    @pl.when(c == last)
    def _():
        for h in range(ATTN_HEADS):
            acc = acc_ref[h]
            o_ref[:, h * dh:(h + 1) * dh] = (acc[:dh] / acc[dh:dh + 1]).T.astype(o_ref.dtype)


def dsa_attention(hq, q_col, k_col, vt, bias, batch, seq_len):
    tq, tk = ATT_TQ, IDX_TK
    nq, nc = seq_len // tq, seq_len // tk
    sub = tq // IDX_TQ
    steps = [(i, c) for i in range(nq) for c in range(((i + 1) * tq - 1) // tk + 1)]
    q_tab = jnp.asarray(np.array([s[0] for s in steps], np.int32))
    c_tab = jnp.asarray(np.array([s[1] for s in steps], np.int32))
    grid_spec = pltpu.PrefetchScalarGridSpec(
        num_scalar_prefetch=2,
        grid=(batch, len(steps)),
        in_specs=[pl.BlockSpec((tq, ATTN_WIDTH), lambda b, s, qt, ct: (b * nq + qt[s], q_col)),
                  pl.BlockSpec((tk, ATTN_WIDTH), lambda b, s, qt, ct: (b * nc + ct[s], k_col)),
                  pl.BlockSpec((ATTN_HEADS, VT_ROWS, tk), lambda b, s, qt, ct: (0, 0, b * nc + ct[s])),
                  pl.BlockSpec((sub, 1, tk, IDX_TQ), lambda b, s, qt, ct: (b * nq + qt[s], ct[s], 0, 0))],
        out_specs=pl.BlockSpec((tq, ATTN_WIDTH), lambda b, s, qt, ct: (b * nq + qt[s], 0)),
        scratch_shapes=[pltpu.VMEM((ATTN_HEADS, 1, tq), f32),
                        pltpu.VMEM((ATTN_HEADS, VT_ROWS, tq), f32),
                        pltpu.VMEM((ATTN_HEADS, tk, tq), f32),
                        pltpu.VMEM((ATTN_HEADS, tk, tq), bf16),
                        pltpu.VMEM((tk, tq), f32)],
    )
    return pl.pallas_call(
        _dsa_attn_body,
        grid_spec=grid_spec,
        out_shape=jax.ShapeDtypeStruct((batch * seq_len, ATTN_WIDTH), bf16),
        compiler_params=_cparams(("parallel", "arbitrary")),
        name="dsa_attention",
    )(q_tab, c_tab, hq, hq, vt, bias)


def _out_proj_ln_body(a_ref, o_ref, w_ref, x_ref, g_ref, b_ref, y_ref):
    acc = jnp.dot(a_ref[...], w_ref[:POOL_WIDTH, :], preferred_element_type=f32)
    acc = acc + jnp.dot(o_ref[...], w_ref[POOL_WIDTH:, :], preferred_element_type=f32)
    y_ref[...] = _layer_norm(ALPHA * x_ref[...] + acc, g_ref[...], b_ref[...])


def out_proj_ln(a, o, w_out_b, x, g, b):
    m = x.shape[0]
    tm = ROW_TM
    return pl.pallas_call(
        _out_proj_ln_body,
        grid=(m // tm,),
        in_specs=[pl.BlockSpec((tm, POOL_WIDTH), lambda i: (i, 0)),
                  pl.BlockSpec((tm, ATTN_WIDTH), lambda i: (i, 0)),
                  pl.BlockSpec((POOL_WIDTH + ATTN_WIDTH, D_MODEL), lambda i: (0, 0)),
                  pl.BlockSpec((tm, D_MODEL), lambda i: (i, 0)),
                  pl.BlockSpec((1, D_MODEL), lambda i: (0, 0)),
                  pl.BlockSpec((1, D_MODEL), lambda i: (0, 0))],
        out_specs=pl.BlockSpec((tm, D_MODEL), lambda i: (i, 0)),
        out_shape=jax.ShapeDtypeStruct((m, D_MODEL), f32),
        compiler_params=_cparams(("parallel",)),
        name="out_proj_ln",
    )(a, o, w_out_b, x, g.reshape(1, -1), b.reshape(1, -1))


def _conv_out_ln_body(gb_ref, gc_ref, hv_ref, gch_ref, hvh_ref, cw_ref, w_ref, x_ref, g_ref, b_ref, y_ref,
                      *, tiles_per_seq):
    tm = gb_ref.shape[0]
    first = (pl.program_id(0) % tiles_per_seq) == 0
    u = gc_ref[...].astype(f32) * hv_ref[...].astype(f32)
    uh = gch_ref[...].astype(f32) * hvh_ref[...].astype(f32)
    uh = jnp.where(first, 0.0, uh)
    hr = uh.shape[0]
    row = lax.broadcasted_iota(i32, (tm, 1), 0)
    u1 = jnp.where(row == 0, uh[hr - 1:hr, :], pltpu.roll(u, 1, 0))
    u2 = jnp.where(row == 0, uh[hr - 2:hr - 1, :],
                   jnp.where(row == 1, uh[hr - 1:hr, :], pltpu.roll(u, 2, 0)))
    z = cw_ref[0:1, :] * u2 + cw_ref[1:2, :] * u1 + cw_ref[2:3, :] * u
    gz = (gb_ref[...].astype(f32) * z).astype(bf16)
    acc = jnp.dot(gz, w_ref[...], preferred_element_type=f32)
    y_ref[...] = _layer_norm(ALPHA * x_ref[...] + acc, g_ref[...], b_ref[...])


def conv_out_ln(h1, conv_w, w_out_b, x, g, b, seq_len):
    m = x.shape[0]
    tm = ROW_TM
    halo = 16
    hb = tm // halo

    def halo_idx(col):
        return lambda i: (jnp.maximum(i * hb - 1, 0), col)

    return pl.pallas_call(
        functools.partial(_conv_out_ln_body, tiles_per_seq=seq_len // tm),
        grid=(m // tm,),
        in_specs=[pl.BlockSpec((tm, D_MODEL), lambda i: (i, 0)),
                  pl.BlockSpec((tm, D_MODEL), lambda i: (i, 1)),
                  pl.BlockSpec((tm, D_MODEL), lambda i: (i, 2)),
                  pl.BlockSpec((halo, D_MODEL), halo_idx(1)),
                  pl.BlockSpec((halo, D_MODEL), halo_idx(2)),
                  pl.BlockSpec((CONV_WIDTH, D_MODEL), lambda i: (0, 0)),
                  pl.BlockSpec((D_MODEL, D_MODEL), lambda i: (0, 0)),
                  pl.BlockSpec((tm, D_MODEL), lambda i: (i, 0)),
                  pl.BlockSpec((1, D_MODEL), lambda i: (0, 0)),
                  pl.BlockSpec((1, D_MODEL), lambda i: (0, 0))],
        out_specs=pl.BlockSpec((tm, D_MODEL), lambda i: (i, 0)),
        out_shape=jax.ShapeDtypeStruct((m, D_MODEL), f32),
        compiler_params=_cparams(("parallel",)),
        name="conv_out_ln",
    )(h1, h1, h1, h1, h1, conv_w, w_out_b, x, g.reshape(1, -1), b.reshape(1, -1))


ROUTE_E0, ROUTE_E1, ROUTE_C0, ROUTE_C1, ROUTE_R0, ROUTE_R1 = range(6)


def _split_bf16(x):
    hi = x.astype(bf16)
    lo = (x - hi.astype(f32)).astype(bf16)
    return hi, lo


def _router_body(x_ref, w_ref, b_ref, o_ref, cnt_ref):
    i = pl.program_id(0)

    @pl.when(i == 0)
    def _():
        cnt_ref[...] = jnp.zeros(cnt_ref.shape, f32)

    xh, xl = _split_bf16(x_ref[...])
    wh, wl = _split_bf16(w_ref[...])
    lg = (jnp.dot(xh, wh, preferred_element_type=f32) + jnp.dot(xh, wl, preferred_element_type=f32)
          + jnp.dot(xl, wh, preferred_element_type=f32)) + b_ref[...]
    tm, nl = lg.shape
    lane = lax.broadcasted_iota(i32, (tm, nl), 1)
    neg = -jnp.inf
    is_g = lane < N_GROUPS
    gl = jnp.where(is_g, lg, neg)
    gmax = jnp.max(gl, axis=1, keepdims=True)
    gidx = jnp.min(jnp.where(gl == gmax, lane, nl), axis=1, keepdims=True)
    gw = 1.0 / jnp.sum(jnp.where(is_g, jnp.exp(lg - gmax), 0.0), axis=1, keepdims=True)
    e_lo = N_GROUPS + gidx * EXPERTS_PER_GROUP
    in_grp = (lane >= e_lo) & (lane < e_lo + EXPERTS_PER_GROUP)
    el = jnp.where(in_grp, lg, neg)
    l0 = jnp.max(el, axis=1, keepdims=True)
    i0 = jnp.min(jnp.where(el == l0, lane, nl), axis=1, keepdims=True)
    el1 = jnp.where(lane == i0, neg, el)
    l1 = jnp.max(el1, axis=1, keepdims=True)
    i1 = jnp.min(jnp.where(el1 == l1, lane, nl), axis=1, keepdims=True)
    p1 = jnp.exp(l1 - l0)
    c0 = gw / (1.0 + p1)
    c1 = gw * p1 / (1.0 + p1)
    hot0, hot1 = lane == i0, lane == i1
    onehot = jnp.where(hot0, 1.0, jnp.where(hot1, 1.0, 0.0))
    before = lax.broadcasted_iota(i32, (tm, tm), 0) > lax.broadcasted_iota(i32, (tm, tm), 1)
    seen = jnp.dot(jnp.where(before, 1.0, 0.0).astype(bf16), onehot.astype(bf16),
                   preferred_element_type=f32) + cnt_ref[...]
    r0 = jnp.sum(jnp.where(hot0, seen, 0.0), axis=1, keepdims=True)
    r1 = jnp.sum(jnp.where(hot1, seen, 0.0), axis=1, keepdims=True)
    cnt_ref[...] += jnp.sum(onehot, axis=0, keepdims=True)
    cols = {ROUTE_E0: (i0 - N_GROUPS).astype(f32), ROUTE_E1: (i1 - N_GROUPS).astype(f32),
            ROUTE_C0: c0, ROUTE_C1: c1, ROUTE_R0: r0, ROUTE_R1: r1}
    out = jnp.zeros((tm, nl), f32)
    for k, v in cols.items():
        out = jnp.where(lane == k, v, out)
    o_ref[...] = out


def moe_router(x, w_group, b_group, w_expert, b_expert):
    m = x.shape[0]
    tm = ROW_TM
    nl = LANES
    w = jnp.zeros((D_MODEL, nl), f32).at[:, :N_GROUPS].set(w_group).at[:, N_GROUPS:N_GROUPS + N_EXPERTS].set(w_expert)
    bias = jnp.zeros((1, nl), f32).at[0, :N_GROUPS].set(b_group).at[0, N_GROUPS:N_GROUPS + N_EXPERTS].set(b_expert)
    return pl.pallas_call(
        _router_body,
        grid=(m // tm,),
        in_specs=[pl.BlockSpec((tm, D_MODEL), lambda i: (i, 0)),
                  pl.BlockSpec((D_MODEL, nl), lambda i: (0, 0)),
                  pl.BlockSpec((1, nl), lambda i: (0, 0))],
        out_specs=[pl.BlockSpec((tm, nl), lambda i: (i, 0)),
                   pl.BlockSpec((1, nl), lambda i: (0, 0))],
        out_shape=[jax.ShapeDtypeStruct((m, nl), f32),
                   jax.ShapeDtypeStruct((1, nl), f32)],
        compiler_params=_cparams(("arbitrary",)),
        name="moe_router",
    )(x, w, bias)


def _route_plan(route, counts_row, tm, n_tiles):
    counts = counts_row[0, N_GROUPS:N_GROUPS + N_EXPERTS].astype(i32)
    tiles_per = (counts + tm - 1) // tm
    tile_end = jnp.cumsum(tiles_per)
    tile_start = tile_end - tiles_per
    ri = route[:, :8].astype(i32)
    experts = jnp.arange(N_EXPERTS, dtype=i32)[None, :]

    def slot(e, r):
        return jnp.sum(jnp.where(e[:, None] == experts, tile_start[None, :], 0), axis=1) * tm + r

    pos0 = slot(ri[:, ROUTE_E0], ri[:, ROUTE_R0])
    pos1 = slot(ri[:, ROUTE_E1], ri[:, ROUTE_R1])
    tile_ids = jnp.arange(n_tiles, dtype=i32)
    n_valid = tile_end[-1]
    valid = tile_ids < n_valid
    tile_e = jnp.sum((tile_ids[:, None] >= tile_end[None, :]).astype(i32), axis=1)
    last_e = jnp.sum((n_valid - 1 >= tile_end).astype(i32))
    tile_e = jnp.where(valid, tile_e, last_e).astype(i32)
    prev_e = jnp.concatenate([jnp.full((1,), -1, i32), tile_e[:-1]])
    first = ((tile_e != prev_e) & valid).astype(i32)
    pad_lo = tile_start * tm + counts
    pad_hi = tile_end * tm
    used = tiles_per > 0
    e_ids = jnp.arange(N_EXPERTS, dtype=i32)
    slot_of_e = (jnp.cumsum(used.astype(i32)) - 1) % 2
    later_used = (e_ids[None, :] > e_ids[:, None]) & used[None, :]
    next_of_e = jnp.min(jnp.where(later_used, e_ids[None, :], N_EXPERTS), axis=1)
    next_of_e = jnp.where(next_of_e == N_EXPERTS, -1, next_of_e)
    pick = tile_e[:, None] == e_ids[None, :]
    w_slot = jnp.sum(jnp.where(pick, slot_of_e[None, :], 0), axis=1).astype(i32)
    w_next = jnp.sum(jnp.where(pick, next_of_e[None, :], 0), axis=1).astype(i32)
    return pos0, pos1, pad_lo, pad_hi, tile_e, first, valid.astype(i32), w_slot, w_next


def _start_row_gathers(src_hbm, dst_ref, sem, rows, row_of):
    def body(j, carry):
        for k in range(DMA_UNROLL):
            r = j * DMA_UNROLL + k
            pltpu.make_async_copy(src_hbm.at[pl.ds(row_of(r), 1), :], dst_ref.at[pl.ds(r, 1), :], sem).start()
        return carry
    lax.fori_loop(0, rows // DMA_UNROLL, body, 0)


def _wait_row_gathers(src_hbm, dst_ref, sem):
    rows = dst_ref.shape[0]
    pltpu.make_async_copy(src_hbm.at[pl.ds(0, rows), :], dst_ref, sem).wait()


def _moe_ffn_body(p0_ref, p1_ref, lo_ref, hi_ref, te_ref, first_ref, valid_ref, wslot_ref, wnext_ref,
                  x_hbm, wg_hbm, wu_hbm, wd_hbm, y_ref,
                  src_ref, xg_ref, sem_ref, wst_g, wst_u, wst_d, wsem_ref, wgb_ref, wub_ref, wdb_ref, *, layer):
    tm = MOE_TM
    t = pl.program_id(0)
    slot = t % GATHER_SLOTS
    n_tok = p0_ref.shape[0]
    is_valid = valid_ref[t] == 1
    staged = ((wg_hbm, wst_g, wgb_ref), (wu_hbm, wst_u, wub_ref), (wd_hbm, wst_d, wdb_ref))

    def weight_copy(k, expert, ws):
        hbm, stage, _ = staged[k]
        return pltpu.make_async_copy(hbm.at[layer, expert], stage.at[ws], wsem_ref.at[ws, k])

    @pl.when(t == 0)
    def _():
        for k in range(3):
            weight_copy(k, te_ref[0], wslot_ref[0]).start()
        def fill(n, carry):
            src_ref[p0_ref[n]] = n
            src_ref[p1_ref[n]] = n
            return carry
        lax.fori_loop(0, n_tok, fill, 0, unroll=8)

        def pad(p, carry):
            src_ref[p] = 0
            return carry
        for e in range(N_EXPERTS):
            lax.fori_loop(lo_ref[e], hi_ref[e], pad, 0)
        end = hi_ref[N_EXPERTS - 1]
        lax.fori_loop(end, end + 2 * tm, pad, 0)
        for first_tile in range(2):
            _start_row_gathers(x_hbm, xg_ref.at[first_tile], sem_ref.at[first_tile], tm,
                               lambda r: src_ref[first_tile * tm + r])

    @pl.when(first_ref[t] == 1)
    def _():
        ws = wslot_ref[t]
        for k in range(3):
            weight_copy(k, te_ref[t], ws).wait()
            staged[k][2][...] = staged[k][1][ws].astype(bf16)

        @pl.when(wnext_ref[t] >= 0)
        def _():
            for k in range(3):
                weight_copy(k, wnext_ref[t], 1 - ws).start()

    @pl.when(is_valid)
    def _():
        _wait_row_gathers(x_hbm, xg_ref.at[slot], sem_ref.at[slot])
        nxt = (t + 2) * tm
        nslot = (t + 2) % GATHER_SLOTS

        def prefetch(lo, hi):
            for r in range(lo, hi):
                pltpu.make_async_copy(x_hbm.at[pl.ds(src_ref[nxt + r], 1), :],
                                      xg_ref.at[nslot, pl.ds(r, 1), :], sem_ref.at[nslot]).start()

        xb = xg_ref[slot].astype(bf16)
        gate = jnp.dot(xb, wgb_ref[...], preferred_element_type=f32)
        prefetch(0, tm // 2)
        up = jnp.dot(xb, wub_ref[...], preferred_element_type=f32)
        prefetch(tm // 2, tm)
        hdn = (gate * jax.nn.sigmoid(gate) * up).astype(bf16)
        y_ref[...] = jnp.dot(hdn, wdb_ref[...], preferred_element_type=f32)

    @pl.when(jnp.logical_not(is_valid))
    def _():
        @pl.when(valid_ref[jnp.maximum(t - 1, 0)] == 1)
        def _():
            for ahead in range(2):
                s_late = (t + ahead) % GATHER_SLOTS
                _wait_row_gathers(x_hbm, xg_ref.at[s_late], sem_ref.at[s_late])
        y_ref[...] = jnp.zeros(y_ref.shape, y_ref.dtype)


def moe_ffn(x, plan, w_gate, w_up, w_down, layer, n_tiles):
    tm = MOE_TM
    grid_spec = pltpu.PrefetchScalarGridSpec(
        num_scalar_prefetch=len(plan),
        grid=(n_tiles,),
        in_specs=[pl.BlockSpec(memory_space=pl.ANY)] * 4,
        out_specs=pl.BlockSpec((tm, D_MODEL), lambda t, *pre: (t, 0)),
        scratch_shapes=[pltpu.SMEM((n_tiles * tm,), i32),
                        pltpu.VMEM((GATHER_SLOTS, tm, D_MODEL), f32),
                        pltpu.SemaphoreType.DMA((GATHER_SLOTS,)),
                        pltpu.VMEM((2, D_MODEL, D_FF_EXPERT), f32),
                        pltpu.VMEM((2, D_MODEL, D_FF_EXPERT), f32),
                        pltpu.VMEM((2, D_FF_EXPERT, D_MODEL), f32),
                        pltpu.SemaphoreType.DMA((2, 3)),
                        pltpu.VMEM((D_MODEL, D_FF_EXPERT), bf16),
                        pltpu.VMEM((D_MODEL, D_FF_EXPERT), bf16),
                        pltpu.VMEM((D_FF_EXPERT, D_MODEL), bf16)],
    )
    return pl.pallas_call(
        functools.partial(_moe_ffn_body, layer=layer),
        grid_spec=grid_spec,
        out_shape=jax.ShapeDtypeStruct((n_tiles * tm, D_MODEL), f32),
        compiler_params=_cparams(("arbitrary",)),
        name="moe_ffn",
    )(*plan, x, w_gate, w_up, w_down)


def _combine_ln_body(p0_ref, p1_ref, y_hbm, x_ref, r_ref, g_ref, b_ref, *rest, with_bf16):
    o_ref = rest[0]
    yg_ref, sem_ref = rest[-2:]
    tm = ROW_TM
    t = pl.program_id(0)
    nt = pl.num_programs(0)
    slot = t % GATHER_SLOTS
    nslot = (t + 2) % GATHER_SLOTS
    p_refs = (p0_ref, p1_ref)

    @pl.when(t == 0)
    def _():
        for first_tile in range(2):
            for j in range(2):
                _start_row_gathers(y_hbm, yg_ref.at[first_tile, j], sem_ref.at[first_tile, j], tm,
                                   lambda r: p_refs[j][first_tile * tm + r])

    for j in range(2):
        _wait_row_gathers(y_hbm, yg_ref.at[slot, j], sem_ref.at[slot, j])

    nxt = jnp.minimum(t + 2, nt - 1) * tm

    def prefetch(j, lo, hi):
        for r in range(lo, hi):
            pltpu.make_async_copy(y_hbm.at[pl.ds(p_refs[j][nxt + r], 1), :],
                                  yg_ref.at[nslot, j, pl.ds(r, 1), :], sem_ref.at[nslot, j]).start()

    r = r_ref[...]
    prefetch(0, 0, tm // 2)
    f = r[:, ROUTE_C0:ROUTE_C0 + 1] * yg_ref[slot, 0] + r[:, ROUTE_C1:ROUTE_C1 + 1] * yg_ref[slot, 1]
    prefetch(0, tm // 2, tm)
    z = ALPHA * x_ref[...] + f
    prefetch(1, 0, tm // 2)
    y = _layer_norm(z, g_ref[...], b_ref[...])
    prefetch(1, tm // 2, tm)
    o_ref[...] = y
    if with_bf16:
        rest[1][...] = y.astype(bf16)

    @pl.when(t == nt - 1)
    def _():
        for ahead in (1, 2):
            s_late = (t + ahead) % GATHER_SLOTS
            for j in range(2):
                _wait_row_gathers(y_hbm, yg_ref.at[s_late, j], sem_ref.at[s_late, j])


def combine_ln(y_sorted, pos0, pos1, route, x, g, b, with_bf16):
    m = x.shape[0]
    tm = ROW_TM
    row = lambda t, p0, p1: (t, 0)
    fixed = lambda t, p0, p1: (0, 0)
    out_specs = [pl.BlockSpec((tm, D_MODEL), row)]
    out_shape = [jax.ShapeDtypeStruct((m, D_MODEL), f32)]
    if with_bf16:
        out_specs.append(pl.BlockSpec((tm, D_MODEL), row))
        out_shape.append(jax.ShapeDtypeStruct((m, D_MODEL), bf16))
    grid_spec = pltpu.PrefetchScalarGridSpec(
        num_scalar_prefetch=2,
        grid=(m // tm,),
        in_specs=[pl.BlockSpec(memory_space=pl.ANY),
                  pl.BlockSpec((tm, D_MODEL), row),
                  pl.BlockSpec((tm, LANES), row),
                  pl.BlockSpec((1, D_MODEL), fixed),
                  pl.BlockSpec((1, D_MODEL), fixed)],
        out_specs=out_specs,
        scratch_shapes=[pltpu.VMEM((GATHER_SLOTS, 2, tm, D_MODEL), f32),
                        pltpu.SemaphoreType.DMA((GATHER_SLOTS, 2))],
    )
    return pl.pallas_call(
        functools.partial(_combine_ln_body, with_bf16=with_bf16),
        grid_spec=grid_spec,
        out_shape=out_shape,
        compiler_params=_cparams(("arbitrary",)),
        name="moe_combine_ln",
    )(pos0, pos1, y_sorted, x, route, g.reshape(1, -1), b.reshape(1, -1))


def hier_moe_ln(x, w_group, b_group, w_expert, b_expert, w_gate, w_up, w_down, layer, g, b, with_bf16):
    n = x.shape[0]
    n_tiles = 2 * n // MOE_TM + N_EXPERTS + 2
    route, counts = moe_router(x, w_group, b_group, w_expert, b_expert)
    plan = _route_plan(route, counts, MOE_TM, n_tiles)
    y_sorted = moe_ffn(x, plan, w_gate, w_up, w_down, layer, n_tiles)
    return combine_ln(y_sorted, plan[0], plan[1], route, x, g, b, with_bf16)


def kernel(x, ab_w_in, ab_idx_k_ln_g, ab_idx_k_ln_b, ab_pool_w, ab_pool_scale, ab_w_out, c_w_in, c_conv_w, c_w_out,
           ln_mix_g, ln_mix_b, ln_ffn_g, ln_ffn_b, moe_w_group, moe_b_group, moe_w_expert, moe_b_expert,
           moe_w_gate, moe_w_up, moe_w_down):
    batch, seq_len, d = x.shape
    n = batch * seq_len
    xf = x.reshape(n, d)
    moe = lambda layer: (moe_w_group[layer], moe_b_group[layer], moe_w_expert[layer], moe_b_expert[layer],
                         moe_w_gate, moe_w_up, moe_w_down, layer)

    xb = cast_bf16(xf)
    w_in = ab_w_in[0]
    w_in_b = cast_bf16(w_in)
    hq = matmul_bf16(xb, w_in_b, (0, 1, 2, 4), scales=(1.0, ATTN_HEAD_DIM ** -0.5 * LOG2E, 1.0, 1.0))
    vt = value_t(cast_bf16(w_in[:, V_COL * COL:(V_COL + 1) * COL].T), xb)
    tail = 5 * COL
    w_wi = jnp.pad(w_in[:, tail + IDX_HEAD_DIM:], ((0, 0), (0, LANES - IDX_HEADS)))
    ki, wi = idx_proj(xb, w_in[:, tail:tail + IDX_HEAD_DIM], w_wi, ab_idx_k_ln_g[0], ab_idx_k_ln_b[0])
    a = pool_mixer(hq, ab_pool_w[0], ab_pool_scale[0], seq_len)
    bias = dsa_index(hq, 3, ki, wi, batch, seq_len)
    o = dsa_attention(hq, 1, 2, vt, bias, batch, seq_len)
    x1 = out_proj_ln(a, o, cast_bf16(ab_w_out[0]), xf, ln_mix_g[0], ln_mix_b[0])
    x2, x2b = hier_moe_ln(x1, *moe(0), ln_ffn_g[0], ln_ffn_b[0], True)

    h1 = matmul_bf16(x2b, cast_bf16(c_w_in[0]), tuple(range(3 * D_MODEL // MM_TN)))
    x3 = conv_out_ln(h1, c_conv_w[0], cast_bf16(c_w_out[0]), x2, ln_mix_g[1], ln_mix_b[1], seq_len)
    (x4,) = hier_moe_ln(x3, *moe(1), ln_ffn_g[1], ln_ffn_b[1], False)
    return x4.reshape(batch, seq_len, d)
```

```python
import functools
import math

import numpy as np
import jax
import jax.numpy as jnp
from jax import lax
from jax.experimental import pallas as pl
from jax.experimental.pallas import tpu as pltpu

f32 = jnp.float32
bf16 = jnp.bfloat16
i32 = jnp.int32

D_MODEL = 2048
POOL_WINDOWS = (2, 4, 8, 16)
POOL_WIDTH = 1024
POOL_GROUP = 256
ATTN_HEADS = 8
ATTN_HEAD_DIM = 128
ATTN_WIDTH = 1024
IDX_HEADS = 16
IDX_HEAD_DIM = 64
TOPK_MAX = 256
CONV_WIDTH = 3
N_GROUPS = 4
EXPERTS_PER_GROUP = 8
N_EXPERTS = 32
D_FF_EXPERT = 512
DEPTH = 2
ALPHA = (2 * DEPTH) ** 0.25
LN_EPS = 1e-5
COL = 1024
V_COL = 3

LANES = 128
VMEM_LIMIT = 56 * 1024 * 1024
MM_TM, MM_TN = 512, 1024
ROW_TM = 256
IDX_TQ, IDX_TK = 256, 512
IDX_KS = 128
ATT_TQ = 256
ATT_RB = 128
VT_ROWS = ATTN_HEAD_DIM + 16
MOE_TM = 256
DMA_UNROLL = 8
GATHER_SLOTS = 4
GATHER_AHEAD = GATHER_SLOTS - 1
NEG_BIG = -1e30
INT_MIN = -(2 ** 31)
LOG2E = math.log2(math.e)


def _cparams(sem):
    return pltpu.CompilerParams(dimension_semantics=sem, vmem_limit_bytes=VMEM_LIMIT)


def _layer_norm(z, g, b):
    mu = jnp.mean(z, axis=-1, keepdims=True)
    zc = z - mu
    var = jnp.mean(zc * zc, axis=-1, keepdims=True)
    return zc * lax.rsqrt(var + LN_EPS) * g + b


def _cast_body(x_ref, o_ref):
    o_ref[...] = x_ref[...].astype(o_ref.dtype)


def cast_bf16(x, tm=512):
    m, n = x.shape
    return pl.pallas_call(
        _cast_body,
        grid=(m // tm,),
        in_specs=[pl.BlockSpec((tm, n), lambda i: (i, 0))],
        out_specs=pl.BlockSpec((tm, n), lambda i: (i, 0)),
        out_shape=jax.ShapeDtypeStruct((m, n), bf16),
        compiler_params=_cparams(("parallel",)),
        name="cast_bf16",
    )(x)


def _mm_body(x_ref, w_ref, o_ref, *, scales):
    acc = jnp.dot(x_ref[...], w_ref[...], preferred_element_type=f32)
    j = pl.program_id(0)
    scale = jnp.float32(1.0)
    for jj, s in enumerate(scales):
        if s != 1.0:
            scale = jnp.where(j == jj, jnp.float32(s), scale)
    o_ref[...] = (acc * scale).astype(o_ref.dtype)


def matmul_bf16(x, w, col_blocks, scales=None):
    m, k = x.shape
    nb = len(col_blocks)
    scales = tuple(scales) if scales is not None else (1.0,) * nb
    skip = [c for c in range(col_blocks[0], col_blocks[-1] + 1) if c not in col_blocks]
    assert len(skip) <= 1 and list(col_blocks) == sorted(col_blocks)
    first = col_blocks[0]

    def w_col(j):
        c = j + first
        return c + (c >= skip[0]).astype(i32) if skip else c

    return pl.pallas_call(
        functools.partial(_mm_body, scales=scales),
        grid=(nb, m // MM_TM),
        in_specs=[pl.BlockSpec((MM_TM, k), lambda j, i: (i, 0)),
                  pl.BlockSpec((k, MM_TN), lambda j, i: (0, w_col(j)))],
        out_specs=pl.BlockSpec((MM_TM, MM_TN), lambda j, i: (i, j)),
        out_shape=jax.ShapeDtypeStruct((m, nb * MM_TN), bf16),
        compiler_params=_cparams(("parallel", "parallel")),
        name="proj_matmul",
    )(x, w)


def _value_t_body(wt_ref, x_ref, o_ref):
    vt = lax.dot_general(wt_ref[...], x_ref[...], (((1,), (1,)), ((), ())), preferred_element_type=f32)
    dh = ATTN_HEAD_DIM
    for h in range(ATTN_HEADS):
        o_ref[h, :dh, :] = vt[h * dh:(h + 1) * dh, :].astype(o_ref.dtype)
        o_ref[h, dh:, :] = jnp.ones((VT_ROWS - dh, o_ref.shape[2]), o_ref.dtype)


def value_t(wt, x):
    c, k = wt.shape
    m = x.shape[0]
    return pl.pallas_call(
        _value_t_body,
        grid=(m // MM_TM,),
        in_specs=[pl.BlockSpec((c, k), lambda i: (0, 0)),
                  pl.BlockSpec((MM_TM, k), lambda i: (i, 0))],
        out_specs=pl.BlockSpec((ATTN_HEADS, VT_ROWS, MM_TM), lambda i: (0, 0, i)),
        out_shape=jax.ShapeDtypeStruct((ATTN_HEADS, VT_ROWS, m), bf16),
        compiler_params=_cparams(("parallel",)),
        name="value_t",
    )(wt, x)


def _idx_proj_body(x_ref, wk_ref, ww_ref, g_ref, b_ref, ki_ref, wi_ref, *, wi_scale):
    x = x_ref[...]
    hk = jnp.dot(x, wk_ref[...].astype(bf16), preferred_element_type=f32)
    hw = jnp.dot(x, ww_ref[...].astype(bf16), preferred_element_type=f32)
    ki_ref[...] = _layer_norm(hk, g_ref[...], b_ref[...]).astype(ki_ref.dtype)
    wi_ref[...] = hw * wi_scale


def idx_proj(xb, w_ki, w_wi, g, b):
    m, k = xb.shape
    tm = MM_TM
    wi_scale = (IDX_HEADS ** -0.5) * (IDX_HEAD_DIM ** -0.5)
    return pl.pallas_call(
        functools.partial(_idx_proj_body, wi_scale=wi_scale),
        grid=(m // tm,),
        in_specs=[pl.BlockSpec((tm, k), lambda i: (i, 0)),
                  pl.BlockSpec((k, IDX_HEAD_DIM), lambda i: (0, 0)),
                  pl.BlockSpec((k, LANES), lambda i: (0, 0)),
                  pl.BlockSpec((1, IDX_HEAD_DIM), lambda i: (0, 0)),
                  pl.BlockSpec((1, IDX_HEAD_DIM), lambda i: (0, 0))],
        out_specs=[pl.BlockSpec((tm, IDX_HEAD_DIM), lambda i: (i, 0)),
                   pl.BlockSpec((tm, LANES), lambda i: (i, 0))],
        out_shape=[jax.ShapeDtypeStruct((m, IDX_HEAD_DIM), bf16),
                   jax.ShapeDtypeStruct((m, LANES), f32)],
        compiler_params=_cparams(("parallel",)),
        name="idx_proj",
    )(xb, w_ki, w_wi, g.reshape(1, -1), b.reshape(1, -1))


def _pool_body(up_ref, uc_ref, pw_ref, ps_ref, o_ref, *, tiles_per_seq):
    tm = uc_ref.shape[0]
    i = pl.program_id(0)
    t0 = (i % tiles_per_seq) * tm
    row = lax.broadcasted_iota(i32, (tm, 2 * tm), 0)
    col = lax.broadcasted_iota(i32, (tm, 2 * tm), 1) - tm
    lo_ok = col + t0 >= 0
    pos = t0 + lax.broadcasted_iota(i32, (tm, 1), 0)
    for g, win in enumerate(POOL_WINDOWS):
        cs = slice(g * POOL_GROUP, (g + 1) * POOL_GROUP)
        band = jnp.where((col <= row) & (col > row - win) & lo_ok, 1.0, 0.0).astype(bf16)
        uc = uc_ref[:, cs]
        ucat = jnp.concatenate([up_ref[:, cs], uc], axis=0)
        ssum = jnp.dot(band, ucat, preferred_element_type=f32)
        cnt = jnp.minimum(pos + 1, win).astype(f32)
        pooled = ssum / cnt - uc.astype(f32)
        a = jnp.dot(pooled.astype(bf16), pw_ref[g].astype(bf16), preferred_element_type=f32)
        o_ref[:, cs] = (a * ps_ref[:, cs]).astype(o_ref.dtype)


def pool_mixer(h, pool_w, pool_scale, seq_len):
    m = h.shape[0]
    tm = ROW_TM
    return pl.pallas_call(
        functools.partial(_pool_body, tiles_per_seq=seq_len // tm),
        grid=(m // tm,),
        in_specs=[pl.BlockSpec((tm, POOL_WIDTH), lambda i: (jnp.maximum(i - 1, 0), 0)),
                  pl.BlockSpec((tm, POOL_WIDTH), lambda i: (i, 0)),
                  pl.BlockSpec((len(POOL_WINDOWS), POOL_GROUP, POOL_GROUP), lambda i: (0, 0, 0)),
                  pl.BlockSpec((1, POOL_WIDTH), lambda i: (0, 0))],
        out_specs=pl.BlockSpec((tm, POOL_WIDTH), lambda i: (i, 0)),
        out_shape=jax.ShapeDtypeStruct((m, POOL_WIDTH), bf16),
        compiler_params=_cparams(("parallel",)),
        name="pool_mixer",
    )(h, h, pool_w, pool_scale.reshape(1, -1))


def _sortable(x):
    bits = pltpu.bitcast(x, i32)
    return bits ^ ((bits >> 31) & 0x7FFFFFFF)


def _dsa_index_body(qi_ref, ki_ref, wi_ref, o_ref, key_ref, hi_ref, lo_ref, *, topk):
    tq, tk, ks = IDX_TQ, IDX_TK, IDX_KS
    n_chunks = key_ref.shape[0]
    q0 = pl.program_id(1) * tq
    n_live = (q0 + tq + tk - 1) // tk
    wt = wi_ref[...].T
    qpos = q0 + lax.broadcasted_iota(i32, (1, tq), 1)
    krow = lax.broadcasted_iota(i32, (ks, 1), 0)

    def score_chunk(c, carry):
        k0 = pl.multiple_of(c * tk, tk)
        for s in range(tk // ks):
            kic = ki_ref[pl.ds(k0 + s * ks, ks), :]
            acc = jnp.zeros((ks, tq), f32)
            for h in range(IDX_HEADS):
                qh = qi_ref[:, h * IDX_HEAD_DIM:(h + 1) * IDX_HEAD_DIM]
                sc = lax.dot_general(kic, qh, (((1,), (1,)), ((), ())), preferred_element_type=f32)
                acc = acc + wt[h:h + 1, :] * jnp.maximum(sc, 0.0)
            acc = jnp.where(krow + (k0 + s * ks) <= qpos, acc, -jnp.inf)
            key_ref[c, s * ks:(s + 1) * ks, :] = _sortable(acc)
        return carry

    lax.fori_loop(0, n_live, score_chunk, 0)

    psub = 16
    n_part = 4
    half = 1 << 15

    def count_ge(ref, test):
        test_b = jnp.broadcast_to(test.astype(jnp.int16), (psub, tq))

        def count_chunk(c, parts):
            parts = list(parts)
            for r in range(tk // psub):
                kk = ref[c, r * psub:(r + 1) * psub, :]
                parts[r % n_part] = parts[r % n_part] + jnp.where(kk >= test_b, jnp.int16(1), jnp.int16(0))
            return tuple(parts)

        parts = lax.fori_loop(0, n_live, count_chunk, tuple(jnp.zeros((psub, tq), jnp.int16) for _ in range(n_part)))
        tot = parts[0].astype(i32) + parts[1].astype(i32) + parts[2].astype(i32) + parts[3].astype(i32)
        return jnp.sum(tot, axis=0, keepdims=True)

    def radix_max(ref, need):
        def bit_step(b, cand):
            test = cand | jnp.left_shift(jnp.int32(1), 15 - b)
            cnt = count_ge(ref, test - half)
            return jnp.where(cnt >= need, test, cand)
        return lax.fori_loop(0, 16, bit_step, jnp.zeros((1, tq), i32)) - half

    def split_chunk(c, carry):
        kk = key_ref[c]
        hi_ref[c] = (kk >> 16).astype(jnp.int16)
        lo_ref[c] = ((kk & 0xFFFF) - half).astype(jnp.int16)
        return carry

    lax.fori_loop(0, n_live, split_chunk, 0)
    top = radix_max(hi_ref, topk)
    above = jnp.where(top == half - 1, 0, count_ge(hi_ref, jnp.minimum(top + 1, half - 1)))
    top_b = jnp.broadcast_to(top.astype(jnp.int16), (tk, tq))

    def bucket_chunk(c, carry):
        hi_ref[c] = jnp.where(hi_ref[c] == top_b, lo_ref[c], jnp.int16(-half))
        return carry

    lax.fori_loop(0, n_live, bucket_chunk, 0)
    low = radix_max(hi_ref, topk - above)
    neg_inf_key = INT_MIN + 0x007FFFFF
    thr = jnp.left_shift(top, 16) + (low + half)
    thr = jnp.where(qpos + 1 <= topk, neg_inf_key + 1, jnp.maximum(thr, neg_inf_key + 1))

    def bias_chunk(c, carry):
        o_ref[0, c] = jnp.where(key_ref[c] >= thr, 0.0, NEG_BIG).astype(o_ref.dtype)
        return carry

    lax.fori_loop(0, n_live, bias_chunk, 0)

    def dead_chunk(c, carry):
        o_ref[0, c] = jnp.full((tk, tq), NEG_BIG, o_ref.dtype)
        return carry

    lax.fori_loop(n_live, n_chunks, dead_chunk, 0)


def dsa_index(hq, qi_col, ki, wi, batch, seq_len):
    tq, tk = IDX_TQ, IDX_TK
    nq, nc = seq_len // tq, seq_len // tk
    topk = min(TOPK_MAX, seq_len // 4)
    return pl.pallas_call(
        functools.partial(_dsa_index_body, topk=topk),
        grid=(batch, nq),
        in_specs=[pl.BlockSpec((tq, IDX_HEADS * IDX_HEAD_DIM), lambda b, i: (b * nq + i, qi_col)),
                  pl.BlockSpec((seq_len, IDX_HEAD_DIM), lambda b, i: (b, 0)),
                  pl.BlockSpec((tq, LANES), lambda b, i: (b * nq + i, 0))],
        out_specs=pl.BlockSpec((1, nc, tk, tq), lambda b, i: (b * nq + i, 0, 0, 0)),
        out_shape=jax.ShapeDtypeStruct((batch * nq, nc, tk, tq), bf16),
        scratch_shapes=[pltpu.VMEM((nc, tk, tq), i32),
                        pltpu.VMEM((nc, tk, tq), jnp.int16),
                        pltpu.VMEM((nc, tk, tq), jnp.int16)],
        compiler_params=_cparams(("parallel", "parallel")),
        name="dsa_index",
    )(hq, ki, wi)


def _dsa_attn_body(qt_ref, ct_ref, q_ref, k_ref, vt_ref, bias_ref, o_ref, m_ref, acc_ref, st_ref, p_ref, bf_ref):
    tq, tk, dh = ATT_TQ, IDX_TK, ATTN_HEAD_DIM
    s = pl.program_id(1)
    i, c = qt_ref[s], ct_ref[s]
    last = ((i + 1) * tq - 1) // tk

    @pl.when(c == 0)
    def _():
        m_ref[...] = jnp.full(m_ref.shape, NEG_BIG, f32)
        acc_ref[...] = jnp.zeros(acc_ref.shape, f32)

    rb = ATT_RB
    for j in range(bias_ref.shape[0]):
        bf_ref[:, j * IDX_TQ:(j + 1) * IDX_TQ] = bias_ref[j, 0].astype(f32)
    pending = None
    for h in range(ATTN_HEADS + 1):
        if h < ATTN_HEADS:
            hs = slice(h * dh, (h + 1) * dh)
            qh = q_ref[:, hs]
            m_old = m_ref[h]
            m_blk = m_old
        for r in range(tk // rb):
            rows = slice(r * rb, (r + 1) * rb)
            if h < ATTN_HEADS:
                st = lax.dot_general(k_ref[rows, hs], qh, (((1,), (1,)), ((), ())),
                                     preferred_element_type=f32) + bf_ref[rows, :]
                st_ref[h, rows, :] = st
                m_blk = jnp.maximum(m_blk, jnp.max(st, axis=0, keepdims=True))
            if pending is not None:
                ph, _, pm = pending
                p_ref[ph, rows, :] = jnp.exp2((st_ref[ph, rows, :] - pm).astype(bf16))
        if pending is not None:
            ph, pm_old, pm = pending
            a = jnp.exp2(pm_old - pm)
            acc_ref[ph] = a * acc_ref[ph] + jnp.dot(vt_ref[ph], p_ref[ph], preferred_element_type=f32)
            m_ref[ph] = pm
        pending = (h, m_old, m_blk) if h < ATTN_HEADS else None

    @pl.when(c == last)
    def _():
        for h in range(ATTN_HEADS):
            acc = acc_ref[h]
            o_ref[:, h * dh:(h + 1) * dh] = (acc[:dh] / acc[dh:dh + 1]).T.astype(o_ref.dtype)


def dsa_attention(hq, q_col, k_col, vt, bias, batch, seq_len):
    tq, tk = ATT_TQ, IDX_TK
    nq, nc = seq_len // tq, seq_len // tk
    sub = tq // IDX_TQ
    steps = [(i, c) for i in range(nq) for c in range(((i + 1) * tq - 1) // tk + 1)]
    q_tab = jnp.asarray(np.array([s[0] for s in steps], np.int32))
    c_tab = jnp.asarray(np.array([s[1] for s in steps], np.int32))
    grid_spec = pltpu.PrefetchScalarGridSpec(
        num_scalar_prefetch=2,
        grid=(batch, len(steps)),
        in_specs=[pl.BlockSpec((tq, ATTN_WIDTH), lambda b, s, qt, ct: (b * nq + qt[s], q_col)),
                  pl.BlockSpec((tk, ATTN_WIDTH), lambda b, s, qt, ct: (b * nc + ct[s], k_col)),
                  pl.BlockSpec((ATTN_HEADS, VT_ROWS, tk), lambda b, s, qt, ct: (0, 0, b * nc + ct[s])),
                  pl.BlockSpec((sub, 1, tk, IDX_TQ), lambda b, s, qt, ct: (b * nq + qt[s], ct[s], 0, 0))],
        out_specs=pl.BlockSpec((tq, ATTN_WIDTH), lambda b, s, qt, ct: (b * nq + qt[s], 0)),
        scratch_shapes=[pltpu.VMEM((ATTN_HEADS, 1, tq), f32),
                        pltpu.VMEM((ATTN_HEADS, VT_ROWS, tq), f32),
                        pltpu.VMEM((ATTN_HEADS, tk, tq), f32),
                        pltpu.VMEM((ATTN_HEADS, tk, tq), bf16),
                        pltpu.VMEM((tk, tq), f32)],
    )
    return pl.pallas_call(
        _dsa_attn_body,
        grid_spec=grid_spec,
        out_shape=jax.ShapeDtypeStruct((batch * seq_len, ATTN_WIDTH), bf16),
        compiler_params=_cparams(("parallel", "arbitrary")),
        name="dsa_attention",
    )(q_tab, c_tab, hq, hq, vt, bias)


def _out_proj_ln_body(a_ref, o_ref, w_ref, x_ref, g_ref, b_ref, y_ref):
    acc = jnp.dot(a_ref[...], w_ref[:POOL_WIDTH, :], preferred_element_type=f32)
    acc = acc + jnp.dot(o_ref[...], w_ref[POOL_WIDTH:, :], preferred_element_type=f32)
    y_ref[...] = _layer_norm(ALPHA * x_ref[...] + acc, g_ref[...], b_ref[...])


def out_proj_ln(a, o, w_out_b, x, g, b):
    m = x.shape[0]
    tm = ROW_TM
    return pl.pallas_call(
        _out_proj_ln_body,
        grid=(m // tm,),
        in_specs=[pl.BlockSpec((tm, POOL_WIDTH), lambda i: (i, 0)),
                  pl.BlockSpec((tm, ATTN_WIDTH), lambda i: (i, 0)),
                  pl.BlockSpec((POOL_WIDTH + ATTN_WIDTH, D_MODEL), lambda i: (0, 0)),
                  pl.BlockSpec((tm, D_MODEL), lambda i: (i, 0)),
                  pl.BlockSpec((1, D_MODEL), lambda i: (0, 0)),
                  pl.BlockSpec((1, D_MODEL), lambda i: (0, 0))],
        out_specs=pl.BlockSpec((tm, D_MODEL), lambda i: (i, 0)),
        out_shape=jax.ShapeDtypeStruct((m, D_MODEL), f32),
        compiler_params=_cparams(("parallel",)),
        name="out_proj_ln",
    )(a, o, w_out_b, x, g.reshape(1, -1), b.reshape(1, -1))


def _conv_out_ln_body(gb_ref, gc_ref, hv_ref, gch_ref, hvh_ref, cw_ref, w_ref, x_ref, g_ref, b_ref, y_ref,
                      *, tiles_per_seq):
    tm = gb_ref.shape[0]
    first = (pl.program_id(0) % tiles_per_seq) == 0
    u = gc_ref[...].astype(f32) * hv_ref[...].astype(f32)
    uh = gch_ref[...].astype(f32) * hvh_ref[...].astype(f32)
    uh = jnp.where(first, 0.0, uh)
    hr = uh.shape[0]
    row = lax.broadcasted_iota(i32, (tm, 1), 0)
    u1 = jnp.where(row == 0, uh[hr - 1:hr, :], pltpu.roll(u, 1, 0))
    u2 = jnp.where(row == 0, uh[hr - 2:hr - 1, :],
                   jnp.where(row == 1, uh[hr - 1:hr, :], pltpu.roll(u, 2, 0)))
    z = cw_ref[0:1, :] * u2 + cw_ref[1:2, :] * u1 + cw_ref[2:3, :] * u
    gz = (gb_ref[...].astype(f32) * z).astype(bf16)
    acc = jnp.dot(gz, w_ref[...], preferred_element_type=f32)
    y_ref[...] = _layer_norm(ALPHA * x_ref[...] + acc, g_ref[...], b_ref[...])


def conv_out_ln(h1, conv_w, w_out_b, x, g, b, seq_len):
    m = x.shape[0]
    tm = ROW_TM
    halo = 16
    hb = tm // halo

    def halo_idx(col):
        return lambda i: (jnp.maximum(i * hb - 1, 0), col)

    return pl.pallas_call(
        functools.partial(_conv_out_ln_body, tiles_per_seq=seq_len // tm),
        grid=(m // tm,),
        in_specs=[pl.BlockSpec((tm, D_MODEL), lambda i: (i, 0)),
                  pl.BlockSpec((tm, D_MODEL), lambda i: (i, 1)),
                  pl.BlockSpec((tm, D_MODEL), lambda i: (i, 2)),
                  pl.BlockSpec((halo, D_MODEL), halo_idx(1)),
                  pl.BlockSpec((halo, D_MODEL), halo_idx(2)),
                  pl.BlockSpec((CONV_WIDTH, D_MODEL), lambda i: (0, 0)),
                  pl.BlockSpec((D_MODEL, D_MODEL), lambda i: (0, 0)),
                  pl.BlockSpec((tm, D_MODEL), lambda i: (i, 0)),
                  pl.BlockSpec((1, D_MODEL), lambda i: (0, 0)),
                  pl.BlockSpec((1, D_MODEL), lambda i: (0, 0))],
        out_specs=pl.BlockSpec((tm, D_MODEL), lambda i: (i, 0)),
        out_shape=jax.ShapeDtypeStruct((m, D_MODEL), f32),
        compiler_params=_cparams(("parallel",)),
        name="conv_out_ln",
    )(h1, h1, h1, h1, h1, conv_w, w_out_b, x, g.reshape(1, -1), b.reshape(1, -1))


ROUTE_E0, ROUTE_E1, ROUTE_C0, ROUTE_C1, ROUTE_R0, ROUTE_R1 = range(6)


def _split_bf16(x):
    hi = x.astype(bf16)
    lo = (x - hi.astype(f32)).astype(bf16)
    return hi, lo


def _router_body(x_ref, w_ref, b_ref, o_ref, cnt_ref):
    i = pl.program_id(0)

    @pl.when(i == 0)
    def _():
        cnt_ref[...] = jnp.zeros(cnt_ref.shape, f32)

    xh, xl = _split_bf16(x_ref[...])
    wh, wl = _split_bf16(w_ref[...])
    lg = (jnp.dot(xh, wh, preferred_element_type=f32) + jnp.dot(xh, wl, preferred_element_type=f32)
          + jnp.dot(xl, wh, preferred_element_type=f32)) + b_ref[...]
    tm, nl = lg.shape
    lane = lax.broadcasted_iota(i32, (tm, nl), 1)
    neg = -jnp.inf
    is_g = lane < N_GROUPS
    gl = jnp.where(is_g, lg, neg)
    gmax = jnp.max(gl, axis=1, keepdims=True)
    gidx = jnp.min(jnp.where(gl == gmax, lane, nl), axis=1, keepdims=True)
    gw = 1.0 / jnp.sum(jnp.where(is_g, jnp.exp(lg - gmax), 0.0), axis=1, keepdims=True)
    e_lo = N_GROUPS + gidx * EXPERTS_PER_GROUP
    in_grp = (lane >= e_lo) & (lane < e_lo + EXPERTS_PER_GROUP)
    el = jnp.where(in_grp, lg, neg)
    l0 = jnp.max(el, axis=1, keepdims=True)
    i0 = jnp.min(jnp.where(el == l0, lane, nl), axis=1, keepdims=True)
    el1 = jnp.where(lane == i0, neg, el)
    l1 = jnp.max(el1, axis=1, keepdims=True)
    i1 = jnp.min(jnp.where(el1 == l1, lane, nl), axis=1, keepdims=True)
    p1 = jnp.exp(l1 - l0)
    c0 = gw / (1.0 + p1)
    c1 = gw * p1 / (1.0 + p1)
    hot0, hot1 = lane == i0, lane == i1
    onehot = jnp.where(hot0, 1.0, jnp.where(hot1, 1.0, 0.0))
    before = lax.broadcasted_iota(i32, (tm, tm), 0) > lax.broadcasted_iota(i32, (tm, tm), 1)
    seen = jnp.dot(jnp.where(before, 1.0, 0.0).astype(bf16), onehot.astype(bf16),
                   preferred_element_type=f32) + cnt_ref[...]
    r0 = jnp.sum(jnp.where(hot0, seen, 0.0), axis=1, keepdims=True)
    r1 = jnp.sum(jnp.where(hot1, seen, 0.0), axis=1, keepdims=True)
    cnt_ref[...] += jnp.sum(onehot, axis=0, keepdims=True)
    cols = {ROUTE_E0: (i0 - N_GROUPS).astype(f32), ROUTE_E1: (i1 - N_GROUPS).astype(f32),
            ROUTE_C0: c0, ROUTE_C1: c1, ROUTE_R0: r0, ROUTE_R1: r1}
    out = jnp.zeros((tm, nl), f32)
    for k, v in cols.items():
        out = jnp.where(lane == k, v, out)
    o_ref[...] = out


def moe_router(x, w_group, b_group, w_expert, b_expert):
    m = x.shape[0]
    tm = ROW_TM
    nl = LANES
    w = jnp.zeros((D_MODEL, nl), f32).at[:, :N_GROUPS].set(w_group).at[:, N_GROUPS:N_GROUPS + N_EXPERTS].set(w_expert)
    bias = jnp.zeros((1, nl), f32).at[0, :N_GROUPS].set(b_group).at[0, N_GROUPS:N_GROUPS + N_EXPERTS].set(b_expert)
    return pl.pallas_call(
        _router_body,
        grid=(m // tm,),
        in_specs=[pl.BlockSpec((tm, D_MODEL), lambda i: (i, 0)),
                  pl.BlockSpec((D_MODEL, nl), lambda i: (0, 0)),
                  pl.BlockSpec((1, nl), lambda i: (0, 0))],
        out_specs=[pl.BlockSpec((tm, nl), lambda i: (i, 0)),
                   pl.BlockSpec((1, nl), lambda i: (0, 0))],
        out_shape=[jax.ShapeDtypeStruct((m, nl), f32),
                   jax.ShapeDtypeStruct((1, nl), f32)],
        compiler_params=_cparams(("arbitrary",)),
        name="moe_router",
    )(x, w, bias)


def _route_plan(route, counts_row, tm, n_tiles):
    counts = counts_row[0, N_GROUPS:N_GROUPS + N_EXPERTS].astype(i32)
    tiles_per = (counts + tm - 1) // tm
    tile_end = jnp.cumsum(tiles_per)
    tile_start = tile_end - tiles_per
    ri = route[:, :8].astype(i32)
    experts = jnp.arange(N_EXPERTS, dtype=i32)[None, :]

    def slot(e, r):
        return jnp.sum(jnp.where(e[:, None] == experts, tile_start[None, :], 0), axis=1) * tm + r

    pos0 = slot(ri[:, ROUTE_E0], ri[:, ROUTE_R0])
    pos1 = slot(ri[:, ROUTE_E1], ri[:, ROUTE_R1])
    tile_ids = jnp.arange(n_tiles, dtype=i32)
    n_valid = tile_end[-1]
    valid = tile_ids < n_valid
    tile_e = jnp.sum((tile_ids[:, None] >= tile_end[None, :]).astype(i32), axis=1)
    last_e = jnp.sum((n_valid - 1 >= tile_end).astype(i32))
    tile_e = jnp.where(valid, tile_e, last_e).astype(i32)
    prev_e = jnp.concatenate([jnp.full((1,), -1, i32), tile_e[:-1]])
    first = ((tile_e != prev_e) & valid).astype(i32)
    pad_lo = tile_start * tm + counts
    pad_hi = tile_end * tm
    used = tiles_per > 0
    e_ids = jnp.arange(N_EXPERTS, dtype=i32)
    slot_of_e = (jnp.cumsum(used.astype(i32)) - 1) % 2
    later_used = (e_ids[None, :] > e_ids[:, None]) & used[None, :]
    next_of_e = jnp.min(jnp.where(later_used, e_ids[None, :], N_EXPERTS), axis=1)
    next_of_e = jnp.where(next_of_e == N_EXPERTS, -1, next_of_e)
    pick = tile_e[:, None] == e_ids[None, :]
    w_slot = jnp.sum(jnp.where(pick, slot_of_e[None, :], 0), axis=1).astype(i32)
    w_next = jnp.sum(jnp.where(pick, next_of_e[None, :], 0), axis=1).astype(i32)
    return pos0, pos1, pad_lo, pad_hi, tile_e, first, valid.astype(i32), w_slot, w_next


def _start_row_gathers(src_hbm, dst_ref, sem, rows, row_of):
    def body(j, carry):
        for k in range(DMA_UNROLL):
            r = j * DMA_UNROLL + k
            pltpu.make_async_copy(src_hbm.at[pl.ds(row_of(r), 1), :], dst_ref.at[pl.ds(r, 1), :], sem).start()
        return carry
    lax.fori_loop(0, rows // DMA_UNROLL, body, 0)


def _wait_row_gathers(src_hbm, dst_ref, sem):
    rows = dst_ref.shape[0]
    pltpu.make_async_copy(src_hbm.at[pl.ds(0, rows), :], dst_ref, sem).wait()


def _moe_ffn_body(p0_ref, p1_ref, lo_ref, hi_ref, te_ref, first_ref, valid_ref, wslot_ref, wnext_ref,
                  x_hbm, wg_hbm, wu_hbm, wd_hbm, y_ref,
                  src_ref, xg_ref, sem_ref, wst_g, wst_u, wst_d, wsem_ref, wgb_ref, wub_ref, wdb_ref, *, layer):
    tm = MOE_TM
    t = pl.program_id(0)
    slot = t % GATHER_SLOTS
    n_tok = p0_ref.shape[0]
    is_valid = valid_ref[t] == 1
    staged = ((wg_hbm, wst_g, wgb_ref), (wu_hbm, wst_u, wub_ref), (wd_hbm, wst_d, wdb_ref))

    def weight_copy(k, expert, ws):
        hbm, stage, _ = staged[k]
        return pltpu.make_async_copy(hbm.at[layer, expert], stage.at[ws], wsem_ref.at[ws, k])

    @pl.when(t == 0)
    def _():
        for k in range(3):
            weight_copy(k, te_ref[0], wslot_ref[0]).start()
        def fill(n, carry):
            src_ref[p0_ref[n]] = n
            src_ref[p1_ref[n]] = n
            return carry
        lax.fori_loop(0, n_tok, fill, 0, unroll=8)

        def pad(p, carry):
            src_ref[p] = 0
            return carry
        for e in range(N_EXPERTS):
            lax.fori_loop(lo_ref[e], hi_ref[e], pad, 0)
        end = hi_ref[N_EXPERTS - 1]
        lax.fori_loop(end, end + GATHER_AHEAD * tm, pad, 0)
        for first_tile in range(GATHER_AHEAD):
            _start_row_gathers(x_hbm, xg_ref.at[first_tile], sem_ref.at[first_tile], tm,
                               lambda r: src_ref[first_tile * tm + r])

    @pl.when(first_ref[t] == 1)
    def _():
        ws = wslot_ref[t]
        for k in range(3):
            weight_copy(k, te_ref[t], ws).wait()
            staged[k][2][...] = staged[k][1][ws].astype(bf16)

        @pl.when(wnext_ref[t] >= 0)
        def _():
            for k in range(3):
                weight_copy(k, wnext_ref[t], 1 - ws).start()

    @pl.when(is_valid)
    def _():
        _wait_row_gathers(x_hbm, xg_ref.at[slot], sem_ref.at[slot])
        nxt = (t + GATHER_AHEAD) * tm
        nslot = (t + GATHER_AHEAD) % GATHER_SLOTS

        def prefetch(lo, hi):
            for r in range(lo, hi):
                pltpu.make_async_copy(x_hbm.at[pl.ds(src_ref[nxt + r], 1), :],
                                      xg_ref.at[nslot, pl.ds(r, 1), :], sem_ref.at[nslot]).start()

        xb = xg_ref[slot].astype(bf16)
        gate = jnp.dot(xb, wgb_ref[...], preferred_element_type=f32)
        prefetch(0, tm // 2)
        up = jnp.dot(xb, wub_ref[...], preferred_element_type=f32)
        prefetch(tm // 2, tm)
        hdn = (gate * jax.nn.sigmoid(gate) * up).astype(bf16)
        y_ref[...] = jnp.dot(hdn, wdb_ref[...], preferred_element_type=f32)

    @pl.when(jnp.logical_not(is_valid))
    def _():
        @pl.when(valid_ref[jnp.maximum(t - 1, 0)] == 1)
        def _():
            for ahead in range(GATHER_AHEAD):
                s_late = (t + ahead) % GATHER_SLOTS
                _wait_row_gathers(x_hbm, xg_ref.at[s_late], sem_ref.at[s_late])
        y_ref[...] = jnp.zeros(y_ref.shape, y_ref.dtype)


def moe_ffn(x, plan, w_gate, w_up, w_down, layer, n_tiles):
    tm = MOE_TM
    grid_spec = pltpu.PrefetchScalarGridSpec(
        num_scalar_prefetch=len(plan),
        grid=(n_tiles,),
        in_specs=[pl.BlockSpec(memory_space=pl.ANY)] * 4,
        out_specs=pl.BlockSpec((tm, D_MODEL), lambda t, *pre: (t, 0)),
        scratch_shapes=[pltpu.SMEM((n_tiles * tm,), i32),
                        pltpu.VMEM((GATHER_SLOTS, tm, D_MODEL), f32),
                        pltpu.SemaphoreType.DMA((GATHER_SLOTS,)),
                        pltpu.VMEM((2, D_MODEL, D_FF_EXPERT), f32),
                        pltpu.VMEM((2, D_MODEL, D_FF_EXPERT), f32),
                        pltpu.VMEM((2, D_FF_EXPERT, D_MODEL), f32),
                        pltpu.SemaphoreType.DMA((2, 3)),
                        pltpu.VMEM((D_MODEL, D_FF_EXPERT), bf16),
                        pltpu.VMEM((D_MODEL, D_FF_EXPERT), bf16),
                        pltpu.VMEM((D_FF_EXPERT, D_MODEL), bf16)],
    )
    return pl.pallas_call(
        functools.partial(_moe_ffn_body, layer=layer),
        grid_spec=grid_spec,
        out_shape=jax.ShapeDtypeStruct((n_tiles * tm, D_MODEL), f32),
        compiler_params=_cparams(("arbitrary",)),
        name="moe_ffn",
    )(*plan, x, w_gate, w_up, w_down)


def _combine_ln_body(p0_ref, p1_ref, y_hbm, x_ref, r_ref, g_ref, b_ref, *rest, with_bf16):
    o_ref = rest[0]
    yg_ref, sem_ref = rest[-2:]
    tm = ROW_TM
    t = pl.program_id(0)
    nt = pl.num_programs(0)
    slot = t % GATHER_SLOTS
    nslot = (t + GATHER_AHEAD) % GATHER_SLOTS
    p_refs = (p0_ref, p1_ref)

    @pl.when(t == 0)
    def _():
        for first_tile in range(GATHER_AHEAD):
            for j in range(2):
                _start_row_gathers(y_hbm, yg_ref.at[first_tile, j], sem_ref.at[first_tile, j], tm,
                                   lambda r: p_refs[j][first_tile * tm + r])

    for j in range(2):
        _wait_row_gathers(y_hbm, yg_ref.at[slot, j], sem_ref.at[slot, j])

    nxt = jnp.minimum(t + GATHER_AHEAD, nt - 1) * tm

    def prefetch(j, lo, hi):
        for r in range(lo, hi):
            pltpu.make_async_copy(y_hbm.at[pl.ds(p_refs[j][nxt + r], 1), :],
                                  yg_ref.at[nslot, j, pl.ds(r, 1), :], sem_ref.at[nslot, j]).start()

    r = r_ref[...]
    prefetch(0, 0, tm // 2)
    f = r[:, ROUTE_C0:ROUTE_C0 + 1] * yg_ref[slot, 0] + r[:, ROUTE_C1:ROUTE_C1 + 1] * yg_ref[slot, 1]
    prefetch(0, tm // 2, tm)
    z = ALPHA * x_ref[...] + f
    prefetch(1, 0, tm // 2)
    y = _layer_norm(z, g_ref[...], b_ref[...])
    prefetch(1, tm // 2, tm)
    o_ref[...] = y
    if with_bf16:
        rest[1][...] = y.astype(bf16)

    @pl.when(t == nt - 1)
    def _():
        for ahead in range(1, GATHER_AHEAD + 1):
            s_late = (t + ahead) % GATHER_SLOTS
            for j in range(2):
                _wait_row_gathers(y_hbm, yg_ref.at[s_late, j], sem_ref.at[s_late, j])


def combine_ln(y_sorted, pos0, pos1, route, x, g, b, with_bf16):
    m = x.shape[0]
    tm = ROW_TM
    row = lambda t, p0, p1: (t, 0)
    fixed = lambda t, p0, p1: (0, 0)
    out_specs = [pl.BlockSpec((tm, D_MODEL), row)]
    out_shape = [jax.ShapeDtypeStruct((m, D_MODEL), f32)]
    if with_bf16:
        out_specs.append(pl.BlockSpec((tm, D_MODEL), row))
        out_shape.append(jax.ShapeDtypeStruct((m, D_MODEL), bf16))
    grid_spec = pltpu.PrefetchScalarGridSpec(
        num_scalar_prefetch=2,
        grid=(m // tm,),
        in_specs=[pl.BlockSpec(memory_space=pl.ANY),
                  pl.BlockSpec((tm, D_MODEL), row),
                  pl.BlockSpec((tm, LANES), row),
                  pl.BlockSpec((1, D_MODEL), fixed),
                  pl.BlockSpec((1, D_MODEL), fixed)],
        out_specs=out_specs,
        scratch_shapes=[pltpu.VMEM((GATHER_SLOTS, 2, tm, D_MODEL), f32),
                        pltpu.SemaphoreType.DMA((GATHER_SLOTS, 2))],
    )
    return pl.pallas_call(
        functools.partial(_combine_ln_body, with_bf16=with_bf16),
        grid_spec=grid_spec,
        out_shape=out_shape,
        compiler_params=_cparams(("arbitrary",)),
        name="moe_combine_ln",
    )(pos0, pos1, y_sorted, x, route, g.reshape(1, -1), b.reshape(1, -1))


def hier_moe_ln(x, w_group, b_group, w_expert, b_expert, w_gate, w_up, w_down, layer, g, b, with_bf16):
    n = x.shape[0]
    n_tiles = 2 * n // MOE_TM + N_EXPERTS + GATHER_AHEAD
    route, counts = moe_router(x, w_group, b_group, w_expert, b_expert)
    plan = _route_plan(route, counts, MOE_TM, n_tiles)
    y_sorted = moe_ffn(x, plan, w_gate, w_up, w_down, layer, n_tiles)
    return combine_ln(y_sorted, plan[0], plan[1], route, x, g, b, with_bf16)


def kernel(x, ab_w_in, ab_idx_k_ln_g, ab_idx_k_ln_b, ab_pool_w, ab_pool_scale, ab_w_out, c_w_in, c_conv_w, c_w_out,
           ln_mix_g, ln_mix_b, ln_ffn_g, ln_ffn_b, moe_w_group, moe_b_group, moe_w_expert, moe_b_expert,
           moe_w_gate, moe_w_up, moe_w_down):
    batch, seq_len, d = x.shape
    n = batch * seq_len
    xf = x.reshape(n, d)
    moe = lambda layer: (moe_w_group[layer], moe_b_group[layer], moe_w_expert[layer], moe_b_expert[layer],
                         moe_w_gate, moe_w_up, moe_w_down, layer)

    xb = cast_bf16(xf)
    w_in = ab_w_in[0]
    w_in_b = cast_bf16(w_in)
    hq = matmul_bf16(xb, w_in_b, (0, 1, 2, 4), scales=(1.0, ATTN_HEAD_DIM ** -0.5 * LOG2E, 1.0, 1.0))
    vt = value_t(cast_bf16(w_in[:, V_COL * COL:(V_COL + 1) * COL].T), xb)
    tail = 5 * COL
    w_wi = jnp.pad(w_in[:, tail + IDX_HEAD_DIM:], ((0, 0), (0, LANES - IDX_HEADS)))
    ki, wi = idx_proj(xb, w_in[:, tail:tail + IDX_HEAD_DIM], w_wi, ab_idx_k_ln_g[0], ab_idx_k_ln_b[0])
    a = pool_mixer(hq, ab_pool_w[0], ab_pool_scale[0], seq_len)
    bias = dsa_index(hq, 3, ki, wi, batch, seq_len)
    o = dsa_attention(hq, 1, 2, vt, bias, batch, seq_len)
    x1 = out_proj_ln(a, o, cast_bf16(ab_w_out[0]), xf, ln_mix_g[0], ln_mix_b[0])
    x2, x2b = hier_moe_ln(x1, *moe(0), ln_ffn_g[0], ln_ffn_b[0], True)

    h1 = matmul_bf16(x2b, cast_bf16(c_w_in[0]), tuple(range(3 * D_MODEL // MM_TN)))
    x3 = conv_out_ln(h1, c_conv_w[0], cast_bf16(c_w_out[0]), x2, ln_mix_g[1], ln_mix_b[1], seq_len)
    (x4,) = hier_moe_ln(x3, *moe(1), ln_ffn_g[1], ln_ffn_b[1], False)
    return x4.reshape(batch, seq_len, d)
```

```python
import functools
import math

import numpy as np
import jax
import jax.numpy as jnp
from jax import lax
from jax.experimental import pallas as pl
from jax.experimental.pallas import tpu as pltpu

f32 = jnp.float32
bf16 = jnp.bfloat16
i32 = jnp.int32

D_MODEL = 2048
POOL_WINDOWS = (2, 4, 8, 16)
POOL_WIDTH = 1024
POOL_GROUP = 256
ATTN_HEADS = 8
ATTN_HEAD_DIM = 128
ATTN_WIDTH = 1024
IDX_HEADS = 16
IDX_HEAD_DIM = 64
TOPK_MAX = 256
CONV_WIDTH = 3
N_GROUPS = 4
EXPERTS_PER_GROUP = 8
N_EXPERTS = 32
D_FF_EXPERT = 512
DEPTH = 2
ALPHA = (2 * DEPTH) ** 0.25
LN_EPS = 1e-5
COL = 1024
V_COL = 3

LANES = 128
VMEM_LIMIT = 56 * 1024 * 1024
MM_TM, MM_TN = 512, 1024
ROW_TM = 256
IDX_TQ, IDX_TK = 256, 512
IDX_KS = 128
ATT_TQ = 256
ATT_RB = 128
VT_ROWS = ATTN_HEAD_DIM + 16
MOE_TM = 256
DMA_UNROLL = 8
GATHER_SLOTS = 3
NEG_BIG = -1e30
INT_MIN = -(2 ** 31)
LOG2E = math.log2(math.e)


def _cparams(sem):
    return pltpu.CompilerParams(dimension_semantics=sem, vmem_limit_bytes=VMEM_LIMIT)


def _layer_norm(z, g, b):
    mu = jnp.mean(z, axis=-1, keepdims=True)
    zc = z - mu
    var = jnp.mean(zc * zc, axis=-1, keepdims=True)
    return zc * lax.rsqrt(var + LN_EPS) * g + b


def _cast_body(x_ref, o_ref):
    o_ref[...] = x_ref[...].astype(o_ref.dtype)


def cast_bf16(x, tm=512):
    m, n = x.shape
    return pl.pallas_call(
        _cast_body,
        grid=(m // tm,),
        in_specs=[pl.BlockSpec((tm, n), lambda i: (i, 0))],
        out_specs=pl.BlockSpec((tm, n), lambda i: (i, 0)),
        out_shape=jax.ShapeDtypeStruct((m, n), bf16),
        compiler_params=_cparams(("parallel",)),
        name="cast_bf16",
    )(x)


def _mm_body(x_ref, w_ref, o_ref, *, scales):
    acc = jnp.dot(x_ref[...], w_ref[...], preferred_element_type=f32)
    j = pl.program_id(0)
    scale = jnp.float32(1.0)
    for jj, s in enumerate(scales):
        if s != 1.0:
            scale = jnp.where(j == jj, jnp.float32(s), scale)
    o_ref[...] = (acc * scale).astype(o_ref.dtype)


def matmul_bf16(x, w, col_blocks, scales=None):
    m, k = x.shape
    nb = len(col_blocks)
    scales = tuple(scales) if scales is not None else (1.0,) * nb
    skip = [c for c in range(col_blocks[0], col_blocks[-1] + 1) if c not in col_blocks]
    assert len(skip) <= 1 and list(col_blocks) == sorted(col_blocks)
    first = col_blocks[0]

    def w_col(j):
        c = j + first
        return c + (c >= skip[0]).astype(i32) if skip else c

    return pl.pallas_call(
        functools.partial(_mm_body, scales=scales),
        grid=(nb, m // MM_TM),
        in_specs=[pl.BlockSpec((MM_TM, k), lambda j, i: (i, 0)),
                  pl.BlockSpec((k, MM_TN), lambda j, i: (0, w_col(j)))],
        out_specs=pl.BlockSpec((MM_TM, MM_TN), lambda j, i: (i, j)),
        out_shape=jax.ShapeDtypeStruct((m, nb * MM_TN), bf16),
        compiler_params=_cparams(("parallel", "parallel")),
        name="proj_matmul",
    )(x, w)


def _value_t_body(wt_ref, x_ref, o_ref):
    vt = lax.dot_general(wt_ref[...], x_ref[...], (((1,), (1,)), ((), ())), preferred_element_type=f32)
    dh = ATTN_HEAD_DIM
    for h in range(ATTN_HEADS):
        o_ref[h, :dh, :] = vt[h * dh:(h + 1) * dh, :].astype(o_ref.dtype)
        o_ref[h, dh:, :] = jnp.ones((VT_ROWS - dh, o_ref.shape[2]), o_ref.dtype)


def value_t(wt, x):
    c, k = wt.shape
    m = x.shape[0]
    return pl.pallas_call(
        _value_t_body,
        grid=(m // MM_TM,),
        in_specs=[pl.BlockSpec((c, k), lambda i: (0, 0)),
                  pl.BlockSpec((MM_TM, k), lambda i: (i, 0))],
        out_specs=pl.BlockSpec((ATTN_HEADS, VT_ROWS, MM_TM), lambda i: (0, 0, i)),
        out_shape=jax.ShapeDtypeStruct((ATTN_HEADS, VT_ROWS, m), bf16),
        compiler_params=_cparams(("parallel",)),
        name="value_t",
    )(wt, x)


def _idx_proj_body(x_ref, wk_ref, ww_ref, g_ref, b_ref, ki_ref, wi_ref, *, wi_scale):
    x = x_ref[...]
    hk = jnp.dot(x, wk_ref[...].astype(bf16), preferred_element_type=f32)
    hw = jnp.dot(x, ww_ref[...].astype(bf16), preferred_element_type=f32)
    ki_ref[...] = _layer_norm(hk, g_ref[...], b_ref[...]).astype(ki_ref.dtype)
    wi_ref[...] = hw * wi_scale


def idx_proj(xb, w_ki, w_wi, g, b):
    m, k = xb.shape
    tm = MM_TM
    wi_scale = (IDX_HEADS ** -0.5) * (IDX_HEAD_DIM ** -0.5)
    return pl.pallas_call(
        functools.partial(_idx_proj_body, wi_scale=wi_scale),
        grid=(m // tm,),
        in_specs=[pl.BlockSpec((tm, k), lambda i: (i, 0)),
                  pl.BlockSpec((k, IDX_HEAD_DIM), lambda i: (0, 0)),
                  pl.BlockSpec((k, LANES), lambda i: (0, 0)),
                  pl.BlockSpec((1, IDX_HEAD_DIM), lambda i: (0, 0)),
                  pl.BlockSpec((1, IDX_HEAD_DIM), lambda i: (0, 0))],
        out_specs=[pl.BlockSpec((tm, IDX_HEAD_DIM), lambda i: (i, 0)),
                   pl.BlockSpec((tm, LANES), lambda i: (i, 0))],
        out_shape=[jax.ShapeDtypeStruct((m, IDX_HEAD_DIM), bf16),
                   jax.ShapeDtypeStruct((m, LANES), f32)],
        compiler_params=_cparams(("parallel",)),
        name="idx_proj",
    )(xb, w_ki, w_wi, g.reshape(1, -1), b.reshape(1, -1))


def _pool_body(up_ref, uc_ref, pw_ref, ps_ref, o_ref, *, tiles_per_seq):
    tm = uc_ref.shape[0]
    i = pl.program_id(0)
    t0 = (i % tiles_per_seq) * tm
    row = lax.broadcasted_iota(i32, (tm, 2 * tm), 0)
    col = lax.broadcasted_iota(i32, (tm, 2 * tm), 1) - tm
    lo_ok = col + t0 >= 0
    pos = t0 + lax.broadcasted_iota(i32, (tm, 1), 0)
    for g, win in enumerate(POOL_WINDOWS):
        cs = slice(g * POOL_GROUP, (g + 1) * POOL_GROUP)
        band = jnp.where((col <= row) & (col > row - win) & lo_ok, 1.0, 0.0).astype(bf16)
        uc = uc_ref[:, cs]
        ucat = jnp.concatenate([up_ref[:, cs], uc], axis=0)
        ssum = jnp.dot(band, ucat, preferred_element_type=f32)
        cnt = jnp.minimum(pos + 1, win).astype(f32)
        pooled = ssum / cnt - uc.astype(f32)
        a = jnp.dot(pooled.astype(bf16), pw_ref[g].astype(bf16), preferred_element_type=f32)
        o_ref[:, cs] = (a * ps_ref[:, cs]).astype(o_ref.dtype)


def pool_mixer(h, pool_w, pool_scale, seq_len):
    m = h.shape[0]
    tm = ROW_TM
    return pl.pallas_call(
        functools.partial(_pool_body, tiles_per_seq=seq_len // tm),
        grid=(m // tm,),
        in_specs=[pl.BlockSpec((tm, POOL_WIDTH), lambda i: (jnp.maximum(i - 1, 0), 0)),
                  pl.BlockSpec((tm, POOL_WIDTH), lambda i: (i, 0)),
                  pl.BlockSpec((len(POOL_WINDOWS), POOL_GROUP, POOL_GROUP), lambda i: (0, 0, 0)),
                  pl.BlockSpec((1, POOL_WIDTH), lambda i: (0, 0))],
        out_specs=pl.BlockSpec((tm, POOL_WIDTH), lambda i: (i, 0)),
        out_shape=jax.ShapeDtypeStruct((m, POOL_WIDTH), bf16),
        compiler_params=_cparams(("parallel",)),
        name="pool_mixer",
    )(h, h, pool_w, pool_scale.reshape(1, -1))


def _sortable(x):
    bits = pltpu.bitcast(x, i32)
    return bits ^ ((bits >> 31) & 0x7FFFFFFF)


def _dsa_index_body(qi_ref, ki_ref, wi_ref, o_ref, key_ref, hi_ref, lo_ref, *, topk):
    tq, tk, ks = IDX_TQ, IDX_TK, IDX_KS
    n_chunks = key_ref.shape[0]
    q0 = pl.program_id(1) * tq
    n_live = (q0 + tq + tk - 1) // tk
    wt = wi_ref[...].T
    qpos = q0 + lax.broadcasted_iota(i32, (1, tq), 1)
    krow = lax.broadcasted_iota(i32, (ks, 1), 0)

    def score_chunk(c, carry):
        k0 = pl.multiple_of(c * tk, tk)
        for s in range(tk // ks):
            kic = ki_ref[pl.ds(k0 + s * ks, ks), :]
            acc = jnp.zeros((ks, tq), f32)
            for h in range(IDX_HEADS):
                qh = qi_ref[:, h * IDX_HEAD_DIM:(h + 1) * IDX_HEAD_DIM]
                sc = lax.dot_general(kic, qh, (((1,), (1,)), ((), ())), preferred_element_type=f32)
                acc = acc + wt[h:h + 1, :] * jnp.maximum(sc, 0.0)
            acc = jnp.where(krow + (k0 + s * ks) <= qpos, acc, -jnp.inf)
            key_ref[c, s * ks:(s + 1) * ks, :] = _sortable(acc)
        return carry

    lax.fori_loop(0, n_live, score_chunk, 0)

    psub = 16
    n_part = 4
    half = 1 << 15

    def count_ge(ref, test):
        test_b = jnp.broadcast_to(test.astype(jnp.int16), (psub, tq))

        def count_chunk(c, parts):
            parts = list(parts)
            for r in range(tk // psub):
                kk = ref[c, r * psub:(r + 1) * psub, :]
                parts[r % n_part] = parts[r % n_part] + jnp.where(kk >= test_b, jnp.int16(1), jnp.int16(0))
            return tuple(parts)

        parts = lax.fori_loop(0, n_live, count_chunk, tuple(jnp.zeros((psub, tq), jnp.int16) for _ in range(n_part)))
        tot = parts[0].astype(i32) + parts[1].astype(i32) + parts[2].astype(i32) + parts[3].astype(i32)
        return jnp.sum(tot, axis=0, keepdims=True)

    def radix_max(ref, need):
        def bit_step(b, cand):
            test = cand | jnp.left_shift(jnp.int32(1), 15 - b)
            cnt = count_ge(ref, test - half)
            return jnp.where(cnt >= need, test, cand)
        return lax.fori_loop(0, 16, bit_step, jnp.zeros((1, tq), i32)) - half

    def split_chunk(c, carry):
        kk = key_ref[c]
        hi_ref[c] = (kk >> 16).astype(jnp.int16)
        lo_ref[c] = ((kk & 0xFFFF) - half).astype(jnp.int16)
        return carry

    lax.fori_loop(0, n_live, split_chunk, 0)
    top = radix_max(hi_ref, topk)
    above = jnp.where(top == half - 1, 0, count_ge(hi_ref, jnp.minimum(top + 1, half - 1)))
    top_b = jnp.broadcast_to(top.astype(jnp.int16), (tk, tq))

    def bucket_chunk(c, carry):
        hi_ref[c] = jnp.where(hi_ref[c] == top_b, lo_ref[c], jnp.int16(-half))
        return carry

    lax.fori_loop(0, n_live, bucket_chunk, 0)
    low = radix_max(hi_ref, topk - above)
    neg_inf_key = INT_MIN + 0x007FFFFF
    thr = jnp.left_shift(top, 16) + (low + half)
    thr = jnp.where(qpos + 1 <= topk, neg_inf_key + 1, jnp.maximum(thr, neg_inf_key + 1))

    def bias_chunk(c, carry):
        o_ref[0, c] = jnp.where(key_ref[c] >= thr, 0.0, NEG_BIG).astype(o_ref.dtype)
        return carry

    lax.fori_loop(0, n_live, bias_chunk, 0)

    def dead_chunk(c, carry):
        o_ref[0, c] = jnp.full((tk, tq), NEG_BIG, o_ref.dtype)
        return carry

    lax.fori_loop(n_live, n_chunks, dead_chunk, 0)


def dsa_index(hq, qi_col, ki, wi, batch, seq_len):
    tq, tk = IDX_TQ, IDX_TK
    nq, nc = seq_len // tq, seq_len // tk
    topk = min(TOPK_MAX, seq_len // 4)
    return pl.pallas_call(
        functools.partial(_dsa_index_body, topk=topk),
        grid=(batch, nq),
        in_specs=[pl.BlockSpec((tq, IDX_HEADS * IDX_HEAD_DIM), lambda b, i: (b * nq + i, qi_col)),
                  pl.BlockSpec((seq_len, IDX_HEAD_DIM), lambda b, i: (b, 0)),
                  pl.BlockSpec((tq, LANES), lambda b, i: (b * nq + i, 0))],
        out_specs=pl.BlockSpec((1, nc, tk, tq), lambda b, i: (b * nq + i, 0, 0, 0)),
        out_shape=jax.ShapeDtypeStruct((batch * nq, nc, tk, tq), bf16),
        scratch_shapes=[pltpu.VMEM((nc, tk, tq), i32),
                        pltpu.VMEM((nc, tk, tq), jnp.int16),
                        pltpu.VMEM((nc, tk, tq), jnp.int16)],
        compiler_params=_cparams(("parallel", "parallel")),
        name="dsa_index",
    )(hq, ki, wi)


def _dsa_attn_body(qt_ref, ct_ref, q_ref, k_ref, vt_ref, bias_ref, o_ref, m_ref, acc_ref, st_ref, p_ref, bf_ref):
    tq, tk, dh = ATT_TQ, IDX_TK, ATTN_HEAD_DIM
    s = pl.program_id(1)
    i, c = qt_ref[s], ct_ref[s]
    last = ((i + 1) * tq - 1) // tk

    @pl.when(c == 0)
    def _():
        m_ref[...] = jnp.full(m_ref.shape, NEG_BIG, f32)
        acc_ref[...] = jnp.zeros(acc_ref.shape, f32)

    rb = ATT_RB
    for j in range(bias_ref.shape[0]):
        bf_ref[:, j * IDX_TQ:(j + 1) * IDX_TQ] = bias_ref[j, 0].astype(f32)
    pending = None
    for h in range(ATTN_HEADS + 1):
        if h < ATTN_HEADS:
            hs = slice(h * dh, (h + 1) * dh)
            qh = q_ref[:, hs]
            m_old = m_ref[h]
            m_blk = m_old
        for r in range(tk // rb):
            rows = slice(r * rb, (r + 1) * rb)
            if h < ATTN_HEADS:
                st = lax.dot_general(k_ref[rows, hs], qh, (((1,), (1,)), ((), ())),
                                     preferred_element_type=f32) + bf_ref[rows, :]
                st_ref[h, rows, :] = st
                m_blk = jnp.maximum(m_blk, jnp.max(st, axis=0, keepdims=True))
            if pending is not None:
                ph, _, pm = pending
                p_ref[ph, rows, :] = jnp.exp2((st_ref[ph, rows, :] - pm).astype(bf16))
        if pending is not None:
            ph, pm_old, pm = pending
            a = jnp.exp2(pm_old - pm)
            acc_ref[ph] = a * acc_ref[ph] + jnp.dot(vt_ref[ph], p_ref[ph], preferred_element_type=f32)
            m_ref[ph] = pm
        pending = (h, m_old, m_blk) if h < ATTN_HEADS else None

    @pl.when(c == last)
    def _():
        for h in range(ATTN_HEADS):
            acc = acc_ref[h]
            o_ref[:, h * dh:(h + 1) * dh] = (acc[:dh] / acc[dh:dh + 1]).T.astype(o_ref.dtype)


def dsa_attention(hq, q_col, k_col, vt, bias, batch, seq_len):
    tq, tk = ATT_TQ, IDX_TK
    nq, nc = seq_len // tq, seq_len // tk
    sub = tq // IDX_TQ
    steps = [(i, c) for i in range(nq) for c in range(((i + 1) * tq - 1) // tk + 1)]
    q_tab = jnp.asarray(np.array([s[0] for s in steps], np.int32))
    c_tab = jnp.asarray(np.array([s[1] for s in steps], np.int32))
    grid_spec = pltpu.PrefetchScalarGridSpec(
        num_scalar_prefetch=2,
        grid=(batch, len(steps)),
        in_specs=[pl.BlockSpec((tq, ATTN_WIDTH), lambda b, s, qt, ct: (b * nq + qt[s], q_col)),
                  pl.BlockSpec((tk, ATTN_WIDTH), lambda b, s, qt, ct: (b * nc + ct[s], k_col)),
                  pl.BlockSpec((ATTN_HEADS, VT_ROWS, tk), lambda b, s, qt, ct: (0, 0, b * nc + ct[s])),
                  pl.BlockSpec((sub, 1, tk, IDX_TQ), lambda b, s, qt, ct: (b * nq + qt[s], ct[s], 0, 0))],
        out_specs=pl.BlockSpec((tq, ATTN_WIDTH), lambda b, s, qt, ct: (b * nq + qt[s], 0)),
        scratch_shapes=[pltpu.VMEM((ATTN_HEADS, 1, tq), f32),
                        pltpu.VMEM((ATTN_HEADS, VT_ROWS, tq), f32),
                        pltpu.VMEM((ATTN_HEADS, tk, tq), f32),
                        pltpu.VMEM((ATTN_HEADS, tk, tq), bf16),
                        pltpu.VMEM((tk, tq), f32)],
    )
    return pl.pallas_call(
        _dsa_attn_body,
        grid_spec=grid_spec,
        out_shape=jax.ShapeDtypeStruct((batch * seq_len, ATTN_WIDTH), bf16),
        compiler_params=_cparams(("parallel", "arbitrary")),
        name="dsa_attention",
    )(q_tab, c_tab, hq, hq, vt, bias)


def _out_proj_ln_body(a_ref, o_ref, w_ref, x_ref, g_ref, b_ref, y_ref):
    acc = jnp.dot(a_ref[...], w_ref[:POOL_WIDTH, :], preferred_element_type=f32)
    acc = acc + jnp.dot(o_ref[...], w_ref[POOL_WIDTH:, :], preferred_element_type=f32)
    y_ref[...] = _layer_norm(ALPHA * x_ref[...] + acc, g_ref[...], b_ref[...])


def out_proj_ln(a, o, w_out_b, x, g, b):
    m = x.shape[0]
    tm = ROW_TM
    return pl.pallas_call(
        _out_proj_ln_body,
        grid=(m // tm,),
        in_specs=[pl.BlockSpec((tm, POOL_WIDTH), lambda i: (i, 0)),
                  pl.BlockSpec((tm, ATTN_WIDTH), lambda i: (i, 0)),
                  pl.BlockSpec((POOL_WIDTH + ATTN_WIDTH, D_MODEL), lambda i: (0, 0)),
                  pl.BlockSpec((tm, D_MODEL), lambda i: (i, 0)),
                  pl.BlockSpec((1, D_MODEL), lambda i: (0, 0)),
                  pl.BlockSpec((1, D_MODEL), lambda i: (0, 0))],
        out_specs=pl.BlockSpec((tm, D_MODEL), lambda i: (i, 0)),
        out_shape=jax.ShapeDtypeStruct((m, D_MODEL), f32),
        compiler_params=_cparams(("parallel",)),
        name="out_proj_ln",
    )(a, o, w_out_b, x, g.reshape(1, -1), b.reshape(1, -1))


def _conv_out_ln_body(gb_ref, gc_ref, hv_ref, gch_ref, hvh_ref, cw_ref, w_ref, x_ref, g_ref, b_ref, y_ref,
                      *, tiles_per_seq):
    tm = gb_ref.shape[0]
    first = (pl.program_id(0) % tiles_per_seq) == 0
    u = gc_ref[...].astype(f32) * hv_ref[...].astype(f32)
    uh = gch_ref[...].astype(f32) * hvh_ref[...].astype(f32)
    uh = jnp.where(first, 0.0, uh)
    hr = uh.shape[0]
    row = lax.broadcasted_iota(i32, (tm, 1), 0)
    u1 = jnp.where(row == 0, uh[hr - 1:hr, :], pltpu.roll(u, 1, 0))
    u2 = jnp.where(row == 0, uh[hr - 2:hr - 1, :],
                   jnp.where(row == 1, uh[hr - 1:hr, :], pltpu.roll(u, 2, 0)))
    z = cw_ref[0:1, :] * u2 + cw_ref[1:2, :] * u1 + cw_ref[2:3, :] * u
    gz = (gb_ref[...].astype(f32) * z).astype(bf16)
    acc = jnp.dot(gz, w_ref[...], preferred_element_type=f32)
    y_ref[...] = _layer_norm(ALPHA * x_ref[...] + acc, g_ref[...], b_ref[...])


def conv_out_ln(h1, conv_w, w_out_b, x, g, b, seq_len):
    m = x.shape[0]
    tm = ROW_TM
    halo = 16
    hb = tm // halo

    def halo_idx(col):
        return lambda i: (jnp.maximum(i * hb - 1, 0), col)

    return pl.pallas_call(
        functools.partial(_conv_out_ln_body, tiles_per_seq=seq_len // tm),
        grid=(m // tm,),
        in_specs=[pl.BlockSpec((tm, D_MODEL), lambda i: (i, 0)),
                  pl.BlockSpec((tm, D_MODEL), lambda i: (i, 1)),
                  pl.BlockSpec((tm, D_MODEL), lambda i: (i, 2)),
                  pl.BlockSpec((halo, D_MODEL), halo_idx(1)),
                  pl.BlockSpec((halo, D_MODEL), halo_idx(2)),
                  pl.BlockSpec((CONV_WIDTH, D_MODEL), lambda i: (0, 0)),
                  pl.BlockSpec((D_MODEL, D_MODEL), lambda i: (0, 0)),
                  pl.BlockSpec((tm, D_MODEL), lambda i: (i, 0)),
                  pl.BlockSpec((1, D_MODEL), lambda i: (0, 0)),
                  pl.BlockSpec((1, D_MODEL), lambda i: (0, 0))],
        out_specs=pl.BlockSpec((tm, D_MODEL), lambda i: (i, 0)),
        out_shape=jax.ShapeDtypeStruct((m, D_MODEL), f32),
        compiler_params=_cparams(("parallel",)),
        name="conv_out_ln",
    )(h1, h1, h1, h1, h1, conv_w, w_out_b, x, g.reshape(1, -1), b.reshape(1, -1))


ROUTE_E0, ROUTE_E1, ROUTE_C0, ROUTE_C1, ROUTE_R0, ROUTE_R1 = range(6)


def _split_bf16(x):
    hi = x.astype(bf16)
    lo = (x - hi.astype(f32)).astype(bf16)
    return hi, lo


def _router_body(x_ref, w_ref, b_ref, o_ref, cnt_ref):
    i = pl.program_id(0)

    @pl.when(i == 0)
    def _():
        cnt_ref[...] = jnp.zeros(cnt_ref.shape, f32)

    xh, xl = _split_bf16(x_ref[...])
    wh, wl = _split_bf16(w_ref[...])
    lg = (jnp.dot(xh, wh, preferred_element_type=f32) + jnp.dot(xh, wl, preferred_element_type=f32)
          + jnp.dot(xl, wh, preferred_element_type=f32)) + b_ref[...]
    tm, nl = lg.shape
    lane = lax.broadcasted_iota(i32, (tm, nl), 1)
    neg = -jnp.inf
    is_g = lane < N_GROUPS
    gl = jnp.where(is_g, lg, neg)
    gmax = jnp.max(gl, axis=1, keepdims=True)
    gidx = jnp.min(jnp.where(gl == gmax, lane, nl), axis=1, keepdims=True)
    gw = 1.0 / jnp.sum(jnp.where(is_g, jnp.exp(lg - gmax), 0.0), axis=1, keepdims=True)
    e_lo = N_GROUPS + gidx * EXPERTS_PER_GROUP
    in_grp = (lane >= e_lo) & (lane < e_lo + EXPERTS_PER_GROUP)
    el = jnp.where(in_grp, lg, neg)
    l0 = jnp.max(el, axis=1, keepdims=True)
    i0 = jnp.min(jnp.where(el == l0, lane, nl), axis=1, keepdims=True)
    el1 = jnp.where(lane == i0, neg, el)
    l1 = jnp.max(el1, axis=1, keepdims=True)
    i1 = jnp.min(jnp.where(el1 == l1, lane, nl), axis=1, keepdims=True)
    p1 = jnp.exp(l1 - l0)
    c0 = gw / (1.0 + p1)
    c1 = gw * p1 / (1.0 + p1)
    hot0, hot1 = lane == i0, lane == i1
    onehot = jnp.where(hot0, 1.0, jnp.where(hot1, 1.0, 0.0))
    before = lax.broadcasted_iota(i32, (tm, tm), 0) > lax.broadcasted_iota(i32, (tm, tm), 1)
    seen = jnp.dot(jnp.where(before, 1.0, 0.0).astype(bf16), onehot.astype(bf16),
                   preferred_element_type=f32) + cnt_ref[...]
    r0 = jnp.sum(jnp.where(hot0, seen, 0.0), axis=1, keepdims=True)
    r1 = jnp.sum(jnp.where(hot1, seen, 0.0), axis=1, keepdims=True)
    cnt_ref[...] += jnp.sum(onehot, axis=0, keepdims=True)
    cols = {ROUTE_E0: (i0 - N_GROUPS).astype(f32), ROUTE_E1: (i1 - N_GROUPS).astype(f32),
            ROUTE_C0: c0, ROUTE_C1: c1, ROUTE_R0: r0, ROUTE_R1: r1}
    out = jnp.zeros((tm, nl), f32)
    for k, v in cols.items():
        out = jnp.where(lane == k, v, out)
    o_ref[...] = out


def moe_router(x, w_group, b_group, w_expert, b_expert):
    m = x.shape[0]
    tm = ROW_TM
    nl = LANES
    w = jnp.zeros((D_MODEL, nl), f32).at[:, :N_GROUPS].set(w_group).at[:, N_GROUPS:N_GROUPS + N_EXPERTS].set(w_expert)
    bias = jnp.zeros((1, nl), f32).at[0, :N_GROUPS].set(b_group).at[0, N_GROUPS:N_GROUPS + N_EXPERTS].set(b_expert)
    return pl.pallas_call(
        _router_body,
        grid=(m // tm,),
        in_specs=[pl.BlockSpec((tm, D_MODEL), lambda i: (i, 0)),
                  pl.BlockSpec((D_MODEL, nl), lambda i: (0, 0)),
                  pl.BlockSpec((1, nl), lambda i: (0, 0))],
        out_specs=[pl.BlockSpec((tm, nl), lambda i: (i, 0)),
                   pl.BlockSpec((1, nl), lambda i: (0, 0))],
        out_shape=[jax.ShapeDtypeStruct((m, nl), f32),
                   jax.ShapeDtypeStruct((1, nl), f32)],
        compiler_params=_cparams(("arbitrary",)),
        name="moe_router",
    )(x, w, bias)


def _route_plan(route, counts_row, tm, n_tiles):
    counts = counts_row[0, N_GROUPS:N_GROUPS + N_EXPERTS].astype(i32)
    tiles_per = (counts + tm - 1) // tm
    tile_end = jnp.cumsum(tiles_per)
    tile_start = tile_end - tiles_per
    ri = route[:, :8].astype(i32)
    experts = jnp.arange(N_EXPERTS, dtype=i32)[None, :]

    def slot(e, r):
        return jnp.sum(jnp.where(e[:, None] == experts, tile_start[None, :], 0), axis=1) * tm + r

    pos0 = slot(ri[:, ROUTE_E0], ri[:, ROUTE_R0])
    pos1 = slot(ri[:, ROUTE_E1], ri[:, ROUTE_R1])
    tile_ids = jnp.arange(n_tiles, dtype=i32)
    n_valid = tile_end[-1]
    valid = tile_ids < n_valid
    tile_e = jnp.sum((tile_ids[:, None] >= tile_end[None, :]).astype(i32), axis=1)
    last_e = jnp.sum((n_valid - 1 >= tile_end).astype(i32))
    tile_e = jnp.where(valid, tile_e, last_e).astype(i32)
    prev_e = jnp.concatenate([jnp.full((1,), -1, i32), tile_e[:-1]])
    first = ((tile_e != prev_e) & valid).astype(i32)
    pad_lo = tile_start * tm + counts
    pad_hi = tile_end * tm
    used = tiles_per > 0
    e_ids = jnp.arange(N_EXPERTS, dtype=i32)
    slot_of_e = (jnp.cumsum(used.astype(i32)) - 1) % 2
    later_used = (e_ids[None, :] > e_ids[:, None]) & used[None, :]
    next_of_e = jnp.min(jnp.where(later_used, e_ids[None, :], N_EXPERTS), axis=1)
    next_of_e = jnp.where(next_of_e == N_EXPERTS, -1, next_of_e)
    pick = tile_e[:, None] == e_ids[None, :]
    w_slot = jnp.sum(jnp.where(pick, slot_of_e[None, :], 0), axis=1).astype(i32)
    w_next = jnp.sum(jnp.where(pick, next_of_e[None, :], 0), axis=1).astype(i32)
    return pos0, pos1, pad_lo, pad_hi, tile_e, first, valid.astype(i32), w_slot, w_next


def _start_row_gathers(src_hbm, dst_ref, sem, rows, row_of):
    def body(j, carry):
        for k in range(DMA_UNROLL):
            r = j * DMA_UNROLL + k
            pltpu.make_async_copy(src_hbm.at[pl.ds(row_of(r), 1), :], dst_ref.at[pl.ds(r, 1), :], sem).start()
        return carry
    lax.fori_loop(0, rows // DMA_UNROLL, body, 0)


def _wait_row_gathers(src_hbm, dst_ref, sem):
    rows = dst_ref.shape[0]
    pltpu.make_async_copy(src_hbm.at[pl.ds(0, rows), :], dst_ref, sem).wait()


def _moe_ffn_body(p0_ref, p1_ref, lo_ref, hi_ref, te_ref, first_ref, valid_ref, wslot_ref, wnext_ref,
                  x_hbm, wg_hbm, wu_hbm, wd_hbm, y_ref,
                  src_ref, xg_ref, sem_ref, wst_g, wst_u, wst_d, wsem_ref, wgb_ref, wub_ref, wdb_ref, *, layer):
    tm = MOE_TM
    t = pl.program_id(0)
    slot = t % GATHER_SLOTS
    n_tok = p0_ref.shape[0]
    is_valid = valid_ref[t] == 1
    staged = ((wg_hbm, wst_g, wgb_ref), (wu_hbm, wst_u, wub_ref), (wd_hbm, wst_d, wdb_ref))

    def weight_copy(k, expert, ws):
        hbm, stage, _ = staged[k]
        return pltpu.make_async_copy(hbm.at[layer, expert], stage.at[ws], wsem_ref.at[ws, k])

    @pl.when(t == 0)
    def _():
        for k in range(3):
            weight_copy(k, te_ref[0], wslot_ref[0]).start(priority=1)
        def fill(n, carry):
            src_ref[p0_ref[n]] = n
            src_ref[p1_ref[n]] = n
            return carry
        lax.fori_loop(0, n_tok, fill, 0, unroll=8)

        def pad(p, carry):
            src_ref[p] = 0
            return carry
        for e in range(N_EXPERTS):
            lax.fori_loop(lo_ref[e], hi_ref[e], pad, 0)
        end = hi_ref[N_EXPERTS - 1]
        lax.fori_loop(end, end + 2 * tm, pad, 0)
        for first_tile in range(2):
            _start_row_gathers(x_hbm, xg_ref.at[first_tile], sem_ref.at[first_tile], tm,
                               lambda r: src_ref[first_tile * tm + r])

    @pl.when(first_ref[t] == 1)
    def _():
        ws = wslot_ref[t]
        for k in range(3):
            weight_copy(k, te_ref[t], ws).wait()
            staged[k][2][...] = staged[k][1][ws].astype(bf16)

        @pl.when(wnext_ref[t] >= 0)
        def _():
            for k in range(3):
                weight_copy(k, wnext_ref[t], 1 - ws).start(priority=1)

    @pl.when(is_valid)
    def _():
        _wait_row_gathers(x_hbm, xg_ref.at[slot], sem_ref.at[slot])
        nxt = (t + 2) * tm
        nslot = (t + 2) % GATHER_SLOTS

        def prefetch(lo, hi):
            for r in range(lo, hi):
                pltpu.make_async_copy(x_hbm.at[pl.ds(src_ref[nxt + r], 1), :],
                                      xg_ref.at[nslot, pl.ds(r, 1), :], sem_ref.at[nslot]).start()

        xb = xg_ref[slot].astype(bf16)
        gate = jnp.dot(xb, wgb_ref[...], preferred_element_type=f32)
        prefetch(0, tm // 2)
        up = jnp.dot(xb, wub_ref[...], preferred_element_type=f32)
        prefetch(tm // 2, tm)
        hdn = (gate * jax.nn.sigmoid(gate) * up).astype(bf16)
        y_ref[...] = jnp.dot(hdn, wdb_ref[...], preferred_element_type=f32)

    @pl.when(jnp.logical_not(is_valid))
    def _():
        @pl.when(valid_ref[jnp.maximum(t - 1, 0)] == 1)
        def _():
            for ahead in range(2):
                s_late = (t + ahead) % GATHER_SLOTS
                _wait_row_gathers(x_hbm, xg_ref.at[s_late], sem_ref.at[s_late])
        y_ref[...] = jnp.zeros(y_ref.shape, y_ref.dtype)


def moe_ffn(x, plan, w_gate, w_up, w_down, layer, n_tiles):
    tm = MOE_TM
    grid_spec = pltpu.PrefetchScalarGridSpec(
        num_scalar_prefetch=len(plan),
        grid=(n_tiles,),
        in_specs=[pl.BlockSpec(memory_space=pl.ANY)] * 4,
        out_specs=pl.BlockSpec((tm, D_MODEL), lambda t, *pre: (t, 0)),
        scratch_shapes=[pltpu.SMEM((n_tiles * tm,), i32),
                        pltpu.VMEM((GATHER_SLOTS, tm, D_MODEL), f32),
                        pltpu.SemaphoreType.DMA((GATHER_SLOTS,)),
                        pltpu.VMEM((2, D_MODEL, D_FF_EXPERT), f32),
                        pltpu.VMEM((2, D_MODEL, D_FF_EXPERT), f32),
                        pltpu.VMEM((2, D_FF_EXPERT, D_MODEL), f32),
                        pltpu.SemaphoreType.DMA((2, 3)),
                        pltpu.VMEM((D_MODEL, D_FF_EXPERT), bf16),
                        pltpu.VMEM((D_MODEL, D_FF_EXPERT), bf16),
                        pltpu.VMEM((D_FF_EXPERT, D_MODEL), bf16)],
    )
    return pl.pallas_call(
        functools.partial(_moe_ffn_body, layer=layer),
        grid_spec=grid_spec,
        out_shape=jax.ShapeDtypeStruct((n_tiles * tm, D_MODEL), f32),
        compiler_params=_cparams(("arbitrary",)),
        name="moe_ffn",
    )(*plan, x, w_gate, w_up, w_down)


def _combine_ln_body(p0_ref, p1_ref, y_hbm, x_ref, r_ref, g_ref, b_ref, *rest, with_bf16):
    o_ref = rest[0]
    yg_ref, sem_ref = rest[-2:]
    tm = ROW_TM
    t = pl.program_id(0)
    nt = pl.num_programs(0)
    slot = t % GATHER_SLOTS
    nslot = (t + 2) % GATHER_SLOTS
    p_refs = (p0_ref, p1_ref)

    @pl.when(t == 0)
    def _():
        for first_tile in range(2):
            for j in range(2):
                _start_row_gathers(y_hbm, yg_ref.at[first_tile, j], sem_ref.at[first_tile, j], tm,
                                   lambda r: p_refs[j][first_tile * tm + r])

    for j in range(2):
        _wait_row_gathers(y_hbm, yg_ref.at[slot, j], sem_ref.at[slot, j])

    nxt = jnp.minimum(t + 2, nt - 1) * tm

    def prefetch(j, lo, hi):
        for r in range(lo, hi):
            pltpu.make_async_copy(y_hbm.at[pl.ds(p_refs[j][nxt + r], 1), :],
                                  yg_ref.at[nslot, j, pl.ds(r, 1), :], sem_ref.at[nslot, j]).start(priority=r % 2)

    r = r_ref[...]
    prefetch(0, 0, tm // 2)
    f = r[:, ROUTE_C0:ROUTE_C0 + 1] * yg_ref[slot, 0] + r[:, ROUTE_C1:ROUTE_C1 + 1] * yg_ref[slot, 1]
    prefetch(0, tm // 2, tm)
    z = ALPHA * x_ref[...] + f
    prefetch(1, 0, tm // 2)
    y = _layer_norm(z, g_ref[...], b_ref[...])
    prefetch(1, tm // 2, tm)
    o_ref[...] = y
    if with_bf16:
        rest[1][...] = y.astype(bf16)

    @pl.when(t == nt - 1)
    def _():
        for ahead in (1, 2):
            s_late = (t + ahead) % GATHER_SLOTS
            for j in range(2):
                _wait_row_gathers(y_hbm, yg_ref.at[s_late, j], sem_ref.at[s_late, j])


def combine_ln(y_sorted, pos0, pos1, route, x, g, b, with_bf16):
    m = x.shape[0]
    tm = ROW_TM
    row = lambda t, p0, p1: (t, 0)
    fixed = lambda t, p0, p1: (0, 0)
    out_specs = [pl.BlockSpec((tm, D_MODEL), row)]
    out_shape = [jax.ShapeDtypeStruct((m, D_MODEL), f32)]
    if with_bf16:
        out_specs.append(pl.BlockSpec((tm, D_MODEL), row))
        out_shape.append(jax.ShapeDtypeStruct((m, D_MODEL), bf16))
    grid_spec = pltpu.PrefetchScalarGridSpec(
        num_scalar_prefetch=2,
        grid=(m // tm,),
        in_specs=[pl.BlockSpec(memory_space=pl.ANY),
                  pl.BlockSpec((tm, D_MODEL), row),
                  pl.BlockSpec((tm, LANES), row),
                  pl.BlockSpec((1, D_MODEL), fixed),
                  pl.BlockSpec((1, D_MODEL), fixed)],
        out_specs=out_specs,
        scratch_shapes=[pltpu.VMEM((GATHER_SLOTS, 2, tm, D_MODEL), f32),
                        pltpu.SemaphoreType.DMA((GATHER_SLOTS, 2))],
    )
    return pl.pallas_call(
        functools.partial(_combine_ln_body, with_bf16=with_bf16),
        grid_spec=grid_spec,
        out_shape=out_shape,
        compiler_params=_cparams(("arbitrary",)),
        name="moe_combine_ln",
    )(pos0, pos1, y_sorted, x, route, g.reshape(1, -1), b.reshape(1, -1))


def hier_moe_ln(x, w_group, b_group, w_expert, b_expert, w_gate, w_up, w_down, layer, g, b, with_bf16):
    n = x.shape[0]
    n_tiles = 2 * n // MOE_TM + N_EXPERTS + 2
    route, counts = moe_router(x, w_group, b_group, w_expert, b_expert)
    plan = _route_plan(route, counts, MOE_TM, n_tiles)
    y_sorted = moe_ffn(x, plan, w_gate, w_up, w_down, layer, n_tiles)
    return combine_ln(y_sorted, plan[0], plan[1], route, x, g, b, with_bf16)


def kernel(x, ab_w_in, ab_idx_k_ln_g, ab_idx_k_ln_b, ab_pool_w, ab_pool_scale, ab_w_out, c_w_in, c_conv_w, c_w_out,
           ln_mix_g, ln_mix_b, ln_ffn_g, ln_ffn_b, moe_w_group, moe_b_group, moe_w_expert, moe_b_expert,
           moe_w_gate, moe_w_up, moe_w_down):
    batch, seq_len, d = x.shape
    n = batch * seq_len
    xf = x.reshape(n, d)
    moe = lambda layer: (moe_w_group[layer], moe_b_group[layer], moe_w_expert[layer], moe_b_expert[layer],
                         moe_w_gate, moe_w_up, moe_w_down, layer)

    xb = cast_bf16(xf)
    w_in = ab_w_in[0]
    w_in_b = cast_bf16(w_in)
    hq = matmul_bf16(xb, w_in_b, (0, 1, 2, 4), scales=(1.0, ATTN_HEAD_DIM ** -0.5 * LOG2E, 1.0, 1.0))
    vt = value_t(cast_bf16(w_in[:, V_COL * COL:(V_COL + 1) * COL].T), xb)
    tail = 5 * COL
    w_wi = jnp.pad(w_in[:, tail + IDX_HEAD_DIM:], ((0, 0), (0, LANES - IDX_HEADS)))
    ki, wi = idx_proj(xb, w_in[:, tail:tail + IDX_HEAD_DIM], w_wi, ab_idx_k_ln_g[0], ab_idx_k_ln_b[0])
    a = pool_mixer(hq, ab_pool_w[0], ab_pool_scale[0], seq_len)
    bias = dsa_index(hq, 3, ki, wi, batch, seq_len)
    o = dsa_attention(hq, 1, 2, vt, bias, batch, seq_len)
    x1 = out_proj_ln(a, o, cast_bf16(ab_w_out[0]), xf, ln_mix_g[0], ln_mix_b[0])
    x2, x2b = hier_moe_ln(x1, *moe(0), ln_ffn_g[0], ln_ffn_b[0], True)

    h1 = matmul_bf16(x2b, cast_bf16(c_w_in[0]), tuple(range(3 * D_MODEL // MM_TN)))
    x3 = conv_out_ln(h1, c_conv_w[0], cast_bf16(c_w_out[0]), x2, ln_mix_g[1], ln_mix_b[1], seq_len)
    (x4,) = hier_moe_ln(x3, *moe(1), ln_ffn_g[1], ln_ffn_b[1], False)
    return x4.reshape(batch, seq_len, d)
```
